```python
import math
import jax, jax.numpy as jnp
from jax import lax
import numpy as np

D_MODEL = 1024
BATCH = 8
SEQ = 2048
DEPTH = 1

HEAD_DIM = 64
SWA_Q_HEADS = 8
SWA_KV_HEADS = 2
SWA_WINDOW = 128
SWA_BLOCK = 128
SWA_WIDTH = SWA_Q_HEADS * HEAD_DIM
SWA_KV_WIDTH = SWA_KV_HEADS * HEAD_DIM
MOBA_HEADS = 8
MOBA_BLOCK = 256
MOBA_TOPK = 3
MOBA_QCHUNK = 32
MOBA_WIDTH = MOBA_HEADS * HEAD_DIM
N_HEADS = SWA_Q_HEADS + MOBA_HEADS
MIX_WIDTH = SWA_WIDTH + MOBA_WIDTH
IN_COLS = SWA_WIDTH + 2 * SWA_KV_WIDTH + 3 * MOBA_WIDTH
ATTN_SCALE = HEAD_DIM ** -0.5
REL_BUCKETS = 32
REL_MAX_DIST = 1024
N_EXPERTS = 256
TOP_K = 8
N_GROUPS = 8
TOPK_GROUPS = 4
EXPERT_DIM = 256
SHARED_DIM = 256
ROUTED_SCALE = 2.5
MOE_BLOCK = 64
EPS = 1e-6
NEG_INF = -1e30

kernel_name = "hymba_swa_moba_moe_layer"


def rmsnorm(x, g):
    xf = x.astype(jnp.float32)
    y = xf * lax.rsqrt(jnp.mean(xf * xf, axis=-1, keepdims=True) + EPS)
    return (y * g.astype(jnp.float32)).astype(x.dtype)


def modulate(h, shift, scale):
    return h * (1.0 + scale[:, None, :]) + shift[:, None, :]


def t5_bucket(dist):
    n = jnp.maximum(dist, 0)
    max_exact = REL_BUCKETS // 2
    nf = jnp.maximum(n, 1).astype(jnp.float32)
    large = max_exact + (jnp.log(nf / max_exact) / math.log(REL_MAX_DIST / max_exact)
                         * (REL_BUCKETS - max_exact)).astype(jnp.int32)
    large = jnp.minimum(large, REL_BUCKETS - 1)
    return jnp.where(n < max_exact, n, large)


def swa_attention(q, k, v, sinks, rel_table):
    B, S = q.shape[0], q.shape[1]
    nb = S // SWA_BLOCK
    G = SWA_Q_HEADS // SWA_KV_HEADS
    qb = q.reshape(B, nb, SWA_BLOCK, SWA_KV_HEADS, G, HEAD_DIM)
    kb = k.reshape(B, nb, SWA_BLOCK, SWA_KV_HEADS, HEAD_DIM)
    vb = v.reshape(B, nb, SWA_BLOCK, SWA_KV_HEADS, HEAD_DIM)
    prev = lambda t: jnp.concatenate([jnp.zeros_like(t[:, :1]), t[:, :-1]], axis=1)
    kk = jnp.concatenate([prev(kb), kb], axis=2)
    vv = jnp.concatenate([prev(vb), vb], axis=2)
    qi = jnp.arange(SWA_BLOCK)[:, None]
    kj = jnp.arange(2 * SWA_BLOCK)[None, :]
    dist = qi + SWA_BLOCK - kj
    in_band = (dist >= 0) & (dist < SWA_WINDOW)
    bias = jnp.transpose(rel_table.astype(jnp.float32)[t5_bucket(dist)], (2, 0, 1))
    bias = bias.reshape(SWA_KV_HEADS, G, SWA_BLOCK, 2 * SWA_BLOCK)
    has_prev = (jnp.arange(nb)[:, None, None] > 0) | (kj[None] >= SWA_BLOCK)
    valid = in_band[None] & has_prev
    logits = jnp.einsum('bnqhgd,bnkhd->bnhgqk', qb, kk).astype(jnp.float32) * ATTN_SCALE + bias
    logits = jnp.where(valid[None, :, None, None], logits, NEG_INF)
    sink_col = jnp.broadcast_to(sinks.astype(jnp.float32).reshape(1, 1, SWA_KV_HEADS, G, 1, 1),
                                logits.shape[:-1] + (1,))
    probs = jax.nn.softmax(jnp.concatenate([logits, sink_col], axis=-1), axis=-1)[..., :-1]
    out = jnp.einsum('bnhgqk,bnkhd->bnqhgd', probs.astype(v.dtype), vv)
    return out.reshape(B, S, SWA_WIDTH)


def moba_attention(q, k, v, rel_table):
    B, S, H, dh = q.shape
    Sp = -(-S // MOBA_BLOCK) * MOBA_BLOCK
    pad = ((0, 0), (0, Sp - S), (0, 0), (0, 0))
    q, k, v = jnp.pad(q, pad), jnp.pad(k, pad), jnp.pad(v, pad)
    nb = Sp // MOBA_BLOCK
    kb = k.reshape(B, nb, MOBA_BLOCK, H, dh).transpose(0, 3, 1, 2, 4)
    vb = v.reshape(B, nb, MOBA_BLOCK, H, dh).transpose(0, 3, 1, 2, 4)
    kmean = jnp.mean(kb.astype(jnp.float32), axis=3)
    gate = jnp.einsum('bshd,bhnd->bshn', q.astype(jnp.float32), kmean)
    qblk = jnp.arange(Sp) // MOBA_BLOCK
    past = jnp.arange(nb)[None, :] < qblk[:, None]
    gate = jnp.where(past[None, :, None, :], gate, -jnp.inf)
    k_sel = min(MOBA_TOPK, nb)
    _, sel = lax.top_k(gate, k_sel)
    sel_valid = jnp.arange(k_sel)[None, :] < qblk[:, None]

    nc = Sp // MOBA_QCHUNK
    qc = q.reshape(B, nc, MOBA_QCHUNK, H, dh).transpose(1, 0, 2, 3, 4)
    selc = sel.reshape(B, nc, MOBA_QCHUNK, H, k_sel).transpose(1, 0, 2, 3, 4)
    validc = sel_valid.reshape(nc, MOBA_QCHUNK, k_sel)
    table_h = rel_table.astype(jnp.float32).T
    b_idx = jnp.arange(B)[:, None, None, None]
    h_idx = jnp.arange(H)[None, None, :, None]
    t_off = jnp.arange(MOBA_BLOCK)

    def chunk(args):
        qx, sx, vx, ci = args
        qpos = ci * MOBA_QCHUNK + jnp.arange(MOBA_QCHUNK)
        kg = kb[b_idx, h_idx, sx]
        vg = vb[b_idx, h_idx, sx]
        kpos = sx[..., None] * MOBA_BLOCK + t_off
        dist_g = qpos[None, :, None, None, None] - kpos
        bias_g = table_h[h_idx[..., None], t5_bucket(dist_g)]
        lg = jnp.einsum('bqhd,bqhskd->bqhsk', qx, kg).astype(jnp.float32) * ATTN_SCALE + bias_g
        lg = jnp.where(vx[None, :, None, :, None], lg, NEG_INF)
        ob = (ci * MOBA_QCHUNK) // MOBA_BLOCK
        ko = lax.dynamic_index_in_dim(kb, ob, axis=2, keepdims=False)
        vo = lax.dynamic_index_in_dim(vb, ob, axis=2, keepdims=False)
        dist_o = qpos[:, None] - (ob * MOBA_BLOCK + t_off)[None, :]
        bias_o = table_h[:, t5_bucket(dist_o)].transpose(1, 0, 2)
        lo = jnp.einsum('bqhd,bhkd->bqhk', qx, ko).astype(jnp.float32) * ATTN_SCALE + bias_o[None]
        lo = jnp.where((dist_o >= 0)[None, :, None, :], lo, NEG_INF)
        n_g = k_sel * MOBA_BLOCK
        logits = jnp.concatenate([lg.reshape(B, MOBA_QCHUNK, H, n_g), lo], axis=-1)
        p = jax.nn.softmax(logits, axis=-1).astype(vo.dtype)
        pg = p[..., :n_g].reshape(B, MOBA_QCHUNK, H, k_sel, MOBA_BLOCK)
        return (jnp.einsum('bqhsk,bqhskd->bqhd', pg, vg)
                + jnp.einsum('bqhk,bhkd->bqhd', p[..., n_g:], vo))

    outs = lax.map(chunk, (qc, selc, validc, jnp.arange(nc)))
    return outs.transpose(1, 0, 2, 3, 4).reshape(B, Sp, H * dh)[:, :S]


def moe_ffn(h, w_router, e_bias, w1, w3, w2, ws1, ws3, ws2):
    B, S, D = h.shape
    N = B * S
    x = h.reshape(N, D)
    scores = jax.nn.sigmoid(x.astype(jnp.float32) @ w_router.astype(jnp.float32))
    sel_scores = scores + e_bias.astype(jnp.float32)
    per_group = N_EXPERTS // N_GROUPS
    grp_score = lax.top_k(sel_scores.reshape(N, N_GROUPS, per_group), 2)[0].sum(-1)
    _, gidx = lax.top_k(grp_score, TOPK_GROUPS)
    gmask = jnp.any(gidx[:, :, None] == jnp.arange(N_GROUPS)[None, None, :], axis=1)
    masked = jnp.where(jnp.repeat(gmask, per_group, axis=1), sel_scores, -jnp.inf)
    _, eidx = lax.top_k(masked, TOP_K)
    w = jnp.take_along_axis(scores, eidx, axis=-1)
    w = w / jnp.sum(w, axis=-1, keepdims=True) * ROUTED_SCALE

    A = N * TOP_K
    flat_e = eidx.reshape(A)
    flat_tok = jnp.arange(A, dtype=jnp.int32) // TOP_K
    flat_w = w.reshape(A)
    order = jnp.argsort(flat_e)
    se, stok, sw = flat_e[order], flat_tok[order], flat_w[order]
    counts = jax.ops.segment_sum(jnp.ones((A,), jnp.int32), flat_e, num_segments=N_EXPERTS)
    starts = jnp.cumsum(counts) - counts
    pcounts = (counts + MOE_BLOCK - 1) // MOE_BLOCK * MOE_BLOCK
    pends = jnp.cumsum(pcounts)
    pstarts = pends - pcounts
    dest = pstarts[se] + (jnp.arange(A, dtype=jnp.int32) - starts[se])
    P = A + N_EXPERTS * MOE_BLOCK
    nblk = P // MOE_BLOCK
    buf_tok = jnp.full((P,), N, jnp.int32).at[dest].set(stok)
    buf_w = jnp.zeros((P,), jnp.float32).at[dest].set(sw)
    blk_e = jnp.minimum(jnp.searchsorted(pends, jnp.arange(nblk) * MOE_BLOCK, side='right'),
                        N_EXPERTS - 1)
    xpad = jnp.concatenate([x, jnp.zeros((1, D), x.dtype)], axis=0)

    def expert_rows(args):
        tok, e = args
        xb = xpad[tok]
        return (jax.nn.silu(xb @ w1[e]) * (xb @ w3[e])) @ w2[e]

    ys = lax.map(expert_rows, (buf_tok.reshape(nblk, MOE_BLOCK), blk_e)).reshape(P, D)
    ys = ys * buf_w[:, None].astype(ys.dtype)
    routed = jnp.zeros((N + 1, D), ys.dtype).at[buf_tok].add(ys)[:N]
    shared = (jax.nn.silu(x @ ws1) * (x @ ws3)) @ ws2
    return (routed + shared).reshape(B, S, D)


def setup_inputs(seed: int = 0) -> dict:
    key = jax.random.key(seed)
    ks = jax.random.split(key, 24)
    D, L = D_MODEL, DEPTH
    nrm = lambda k, shape, s: jax.random.normal(k, shape, jnp.float32) * s
    return {
        "x": nrm(ks[0], (BATCH, SEQ, D), 1.0),
        "c": nrm(ks[1], (BATCH, D), 1.0),
        "w_ada": nrm(ks[2], (L, D, 6 * D), 0.5 * D ** -0.5),
        "b_ada": nrm(ks[3], (L, 6 * D), 0.02),
        "norm1_g": 1.0 + nrm(ks[4], (L, D), 0.05),
        "w_in": nrm(ks[5], (L, D, IN_COLS), D ** -0.5),
        "sinks": nrm(ks[6], (L, SWA_Q_HEADS), 1.0),
        "rel_table": nrm(ks[7], (REL_BUCKETS, N_HEADS), 0.3),
        "out_norm_a": 1.0 + nrm(ks[8], (L, SWA_WIDTH), 0.05),
        "out_norm_b": 1.0 + nrm(ks[9], (L, MOBA_WIDTH), 0.05),
        "w_out": nrm(ks[10], (L, MIX_WIDTH, D), MIX_WIDTH ** -0.5),
        "norm2_g": 1.0 + nrm(ks[11], (L, D), 0.05),
        "w_router": nrm(ks[12], (L, D, N_EXPERTS), D ** -0.5),
        "e_bias": nrm(ks[13], (L, N_EXPERTS), 0.01),
        "w1": nrm(ks[14], (L, N_EXPERTS, D, EXPERT_DIM), D ** -0.5),
        "w3": nrm(ks[15], (L, N_EXPERTS, D, EXPERT_DIM), D ** -0.5),
        "w2": nrm(ks[16], (L, N_EXPERTS, EXPERT_DIM, D), EXPERT_DIM ** -0.5),
        "ws1": nrm(ks[17], (L, D, SHARED_DIM), D ** -0.5),
        "ws3": nrm(ks[18], (L, D, SHARED_DIM), D ** -0.5),
        "ws2": nrm(ks[19], (L, SHARED_DIM, D), SHARED_DIM ** -0.5),
        "final_g": 1.0 + nrm(ks[20], (D,), 0.05),
    }


def reference(x, c, w_ada, b_ada, norm1_g, w_in, sinks, rel_table, out_norm_a, out_norm_b,
              w_out, norm2_g, w_router, e_bias, w1, w3, w2, ws1, ws3, ws2, final_g):
    B, S, _ = x.shape
    c_act = jax.nn.silu(c)
    splits = [SWA_WIDTH, SWA_WIDTH + SWA_KV_WIDTH, SWA_WIDTH + 2 * SWA_KV_WIDTH,
              SWA_WIDTH + 2 * SWA_KV_WIDTH + MOBA_WIDTH,
              SWA_WIDTH + 2 * SWA_KV_WIDTH + 2 * MOBA_WIDTH]
    rel_a = rel_table[:, :SWA_Q_HEADS]
    rel_b = rel_table[:, SWA_Q_HEADS:]
    for l in range(DEPTH):
        mod = c_act @ w_ada[l] + b_ada[l]
        sh1, sc1, g1, sh2, sc2, g2 = jnp.split(mod, 6, axis=-1)
        h = modulate(rmsnorm(x, norm1_g[l]), sh1, sc1)
        proj = h @ w_in[l]
        qa, ka, va, qb, kb, vb = jnp.split(proj, splits, axis=-1)
        ya = swa_attention(qa.reshape(B, S, SWA_Q_HEADS, HEAD_DIM),
                           ka.reshape(B, S, SWA_KV_HEADS, HEAD_DIM),
                           va.reshape(B, S, SWA_KV_HEADS, HEAD_DIM), sinks[l], rel_a)
        yb = moba_attention(qb.reshape(B, S, MOBA_HEADS, HEAD_DIM),
                            kb.reshape(B, S, MOBA_HEADS, HEAD_DIM),
                            vb.reshape(B, S, MOBA_HEADS, HEAD_DIM), rel_b)
        y = jnp.concatenate([rmsnorm(ya, out_norm_a[l]), rmsnorm(yb, out_norm_b[l])], axis=-1) @ w_out[l]
        x = x + g1[:, None, :] * y
        h2 = modulate(rmsnorm(x, norm2_g[l]), sh2, sc2)
        x = x + g2[:, None, :] * moe_ffn(h2, w_router[l], e_bias[l], w1[l], w3[l], w2[l],
                                          ws1[l], ws3[l], ws2[l])
    return rmsnorm(x, final_g)
```

```python
import functools
import math

import jax
import jax.numpy as jnp
from jax import lax
from jax.experimental import pallas as pl
from jax.experimental.pallas import tpu as pltpu

D_MODEL = 1024
HEAD_DIM = 64
SWA_Q_HEADS = 8
SWA_KV_HEADS = 2
SWA_GROUP = SWA_Q_HEADS // SWA_KV_HEADS
SWA_WINDOW = 128
SWA_BLOCK = 128
SWA_WIDTH = SWA_Q_HEADS * HEAD_DIM
SWA_KV_WIDTH = SWA_KV_HEADS * HEAD_DIM
MOBA_HEADS = 8
MOBA_BLOCK = 256
MOBA_TOPK = 3
MOBA_WIDTH = MOBA_HEADS * HEAD_DIM
N_HEADS = SWA_Q_HEADS + MOBA_HEADS
MIX_WIDTH = SWA_WIDTH + MOBA_WIDTH
IN_COLS = SWA_WIDTH + 2 * SWA_KV_WIDTH + 3 * MOBA_WIDTH
ATTN_SCALE = HEAD_DIM ** -0.5
REL_BUCKETS = 32
REL_MAX_DIST = 1024
N_EXPERTS = 256
TOP_K = 8
N_GROUPS = 8
TOPK_GROUPS = 4
GROUP_SIZE = N_EXPERTS // N_GROUPS
EXPERT_DIM = 256
SHARED_DIM = 256
ROUTED_SCALE = 2.5
EPS = 1e-6
NEG_INF = -1e30

LANES = 128
SUBLANES = 8
ROW_CHUNKS = D_MODEL // LANES

ADA_TN = 512
INPROJ_TM = 512
OUTPROJ_TM = 256
ROUTE_TN = 512
MOE_TM = 256
COMBINE_TM = 128
VMEM_LIMIT = 48 * 1024 * 1024

_HIGHEST = lax.Precision.HIGHEST
_NT = (((1,), (1,)), ((), ()))


def _silu(a):
    return a * (1.0 / (1.0 + jnp.exp(-a)))


def _t5_bucket(dist):
    n = jnp.maximum(dist, 0)
    max_exact = REL_BUCKETS // 2
    nf = jnp.maximum(n, 1).astype(jnp.float32)
    large = max_exact + (jnp.log(nf / max_exact) / math.log(REL_MAX_DIST / max_exact)
                         * (REL_BUCKETS - max_exact)).astype(jnp.int32)
    large = jnp.minimum(large, REL_BUCKETS - 1)
    return jnp.where(n < max_exact, n, large)


def _ada_kernel(c_ref, w_ref, b_ref, o_ref):
    c = c_ref[...]
    o_ref[...] = jnp.dot(_silu(c), w_ref[...], precision=_HIGHEST,
                         preferred_element_type=jnp.float32) + b_ref[...]


def _ada_call(c, w_ada, b_ada):
    B, D = c.shape
    n_out = w_ada.shape[1]
    return pl.pallas_call(
        _ada_kernel,
        grid=(n_out // ADA_TN,),
        in_specs=[pl.BlockSpec((B, D), lambda j: (0, 0)),
                  pl.BlockSpec((D, ADA_TN), lambda j: (0, j)),
                  pl.BlockSpec((1, ADA_TN), lambda j: (0, j))],
        out_specs=pl.BlockSpec((B, ADA_TN), lambda j: (0, j)),
        out_shape=jax.ShapeDtypeStruct((B, n_out), jnp.float32),
        name="adaln",
    )(c, w_ada, b_ada.reshape(1, n_out))


def _inproj_kernel(x_ref, g_ref, sh_ref, sc_ref, w_ref, qa_ref, ka_ref, va_ref, qb_ref, kb_ref, vb_ref):
    x = x_ref[...]
    h = x * lax.rsqrt(jnp.mean(x * x, axis=-1, keepdims=True) + EPS) * g_ref[...]
    h = h * (1.0 + sc_ref[0]) + sh_ref[0]
    p = jnp.dot(h.astype(jnp.bfloat16), w_ref[...], preferred_element_type=jnp.float32)
    off = 0
    for ref in (qa_ref, ka_ref, va_ref, qb_ref, kb_ref, vb_ref):
        width = ref.shape[-1]
        ref[...] = p[:, off:off + width].astype(ref.dtype)
        off += width


def _inproj_call(x2, g, sh, sc, w_bf16, seq):
    n_tok, D = x2.shape
    tm = INPROJ_TM
    per_seq = seq // tm
    widths = (SWA_WIDTH, SWA_KV_WIDTH, SWA_KV_WIDTH, MOBA_WIDTH, MOBA_WIDTH, MOBA_WIDTH)
    mod_spec = pl.BlockSpec((1, 1, D), lambda i: (i // per_seq, 0, 0))
    return pl.pallas_call(
        _inproj_kernel,
        grid=(n_tok // tm,),
        in_specs=[pl.BlockSpec((tm, D), lambda i: (i, 0)),
                  pl.BlockSpec((1, D), lambda i: (0, 0)),
                  mod_spec, mod_spec,
                  pl.BlockSpec((D, IN_COLS), lambda i: (0, 0))],
        out_specs=[pl.BlockSpec((tm, w), lambda i: (i, 0)) for w in widths],
        out_shape=[jax.ShapeDtypeStruct((n_tok, w), jnp.bfloat16) for w in widths],
        compiler_params=pltpu.CompilerParams(vmem_limit_bytes=VMEM_LIMIT),
        name="inproj",
    )(x2, g, sh, sc, w_bf16)


def _swa_kernel(q_ref, kc_ref, kp_ref, vc_ref, vp_ref, bias_ref, sink_ref, o_ref):
    n = pl.program_id(1)
    rows = SWA_GROUP * SWA_BLOCK
    lo = lax.broadcasted_iota(jnp.int32, (1, LANES), 1) < HEAD_DIM
    kk = jnp.concatenate([kp_ref[0], kc_ref[0]], axis=0)
    vv = jnp.concatenate([vp_ref[0], vc_ref[0]], axis=0)
    qi = lax.broadcasted_iota(jnp.int32, (rows, 2 * SWA_BLOCK), 0) & (SWA_BLOCK - 1)
    kj = lax.broadcasted_iota(jnp.int32, (rows, 2 * SWA_BLOCK), 1)
    dist = qi + SWA_BLOCK - kj
    valid = (dist >= 0) & (dist < SWA_WINDOW) & ((kj >= SWA_BLOCK) | (n > 0))
    qs = jnp.concatenate([q_ref[0, :, c * LANES:(c + 1) * LANES] for c in range(SWA_GROUP)], axis=0)
    outs = []
    for kvh in range(SWA_KV_HEADS):
        hm = lo if kvh == 0 else jnp.logical_not(lo)
        qm = jnp.where(hm, qs, jnp.zeros_like(qs))
        s = lax.dot_general(qm, kk, _NT, preferred_element_type=jnp.float32)
        s = s * ATTN_SCALE + bias_ref[kvh]
        s = jnp.where(valid, s, NEG_INF)
        sink = sink_ref[kvh]
        m = jnp.maximum(jnp.max(s, axis=-1, keepdims=True), sink)
        p = jnp.exp(s - m)
        vm = jnp.where(hm, vv, jnp.ones_like(vv))
        acc = jnp.dot(p.astype(jnp.bfloat16), vm, preferred_element_type=jnp.float32)
        denom = pltpu.roll(acc, HEAD_DIM, 1) + jnp.exp(sink - m)
        outs.append(acc / denom)
    for c in range(SWA_GROUP):
        blk = jnp.where(lo, outs[0][c * SWA_BLOCK:(c + 1) * SWA_BLOCK],
                        outs[1][c * SWA_BLOCK:(c + 1) * SWA_BLOCK])
        o_ref[0, :, c * LANES:(c + 1) * LANES] = blk.astype(o_ref.dtype)


def _swa_call(qa, ka, va, bias_sw, sink_col):
    B, S, _ = qa.shape
    nb = S // SWA_BLOCK
    rows = SWA_GROUP * SWA_BLOCK
    cur = lambda b, n: (b, n, 0)
    prev = lambda b, n: (b, jnp.maximum(n - 1, 0), 0)
    kv_blk = (1, SWA_BLOCK, SWA_KV_WIDTH)
    return pl.pallas_call(
        _swa_kernel,
        grid=(B, nb),
        in_specs=[pl.BlockSpec((1, SWA_BLOCK, SWA_WIDTH), cur),
                  pl.BlockSpec(kv_blk, cur), pl.BlockSpec(kv_blk, prev),
                  pl.BlockSpec(kv_blk, cur), pl.BlockSpec(kv_blk, prev),
                  pl.BlockSpec((SWA_KV_HEADS, rows, 2 * SWA_BLOCK), lambda b, n: (0, 0, 0)),
                  pl.BlockSpec((SWA_KV_HEADS, rows, 1), lambda b, n: (0, 0, 0))],
        out_specs=pl.BlockSpec((1, SWA_BLOCK, SWA_WIDTH), cur),
        out_shape=jax.ShapeDtypeStruct((B, S, SWA_WIDTH), jnp.bfloat16),
        compiler_params=pltpu.CompilerParams(vmem_limit_bytes=VMEM_LIMIT),
        name="swa",
    )(qa, ka, ka, va, va, bias_sw, sink_col)


def _moba_kernel(q_ref, k_ref, v_ref, bias_ref, o_ref):
    S = q_ref.shape[1]
    nb = S // MOBA_BLOCK
    blk = MOBA_BLOCK
    lo = lax.broadcasted_iota(jnp.int32, (1, LANES), 1) < HEAD_DIM
    kmeans = [jnp.mean(k_ref[0, j * blk:(j + 1) * blk, :].astype(jnp.float32), axis=0, keepdims=True)
              for j in range(nb)]
    kmean = jnp.concatenate(kmeans + [jnp.zeros((LANES - nb, LANES), jnp.float32)], axis=0)
    causal = (lax.broadcasted_iota(jnp.int32, (blk, blk), 1)
              <= lax.broadcasted_iota(jnp.int32, (blk, blk), 0))
    lane_id = lax.broadcasted_iota(jnp.int32, (blk, LANES), 1)

    def q_block(i, carry):
        row0 = pl.multiple_of(i * blk, blk)
        q = q_ref[0, pl.ds(row0, blk), :]
        kd = k_ref[0, pl.ds(row0, blk), :]
        vd = v_ref[0, pl.ds(row0, blk), :]
        res = []
        for hh in range(2):
            hm = lo if hh == 0 else jnp.logical_not(lo)
            qm = jnp.where(hm, q, jnp.zeros_like(q))
            gate = lax.dot_general(qm.astype(jnp.float32), kmean, _NT, precision=_HIGHEST,
                                   preferred_element_type=jnp.float32)
            rank = jnp.zeros((blk, LANES), jnp.float32)
            for jp in range(nb):
                gj = gate[:, jp:jp + 1]
                beats = (gj > gate) | ((gj == gate) & (jp < lane_id))
                rank = rank + jnp.where(beats & (jp < i), 1.0, 0.0)
            sel = jnp.where(rank < MOBA_TOPK, 1.0, 0.0)
            s = lax.dot_general(qm, kd, _NT, preferred_element_type=jnp.float32)
            s = s * ATTN_SCALE + bias_ref[hh, 0]
            s = jnp.where(causal, s, NEG_INF)
            m = jnp.max(s, axis=-1, keepdims=True)
            p = jnp.exp(s - m)
            acc = jnp.dot(p.astype(jnp.bfloat16), jnp.where(hm, vd, jnp.ones_like(vd)),
                          preferred_element_type=jnp.float32)

            def past_block(j, mc, qm=qm, hm=hm, hh=hh, sel=sel):
                m, acc = mc
                r0 = pl.multiple_of(j * blk, blk)
                kj = k_ref[0, pl.ds(r0, blk), :]
                vj = v_ref[0, pl.ds(r0, blk), :]
                s = lax.dot_general(qm, kj, _NT, preferred_element_type=jnp.float32)
                s = s * ATTN_SCALE + bias_ref[hh, i - j]
                picked = jnp.sum(jnp.where(lane_id == j, sel, 0.0), axis=-1, keepdims=True) > 0.5
                s = jnp.where(picked, s, NEG_INF)
                m_new = jnp.maximum(m, jnp.max(s, axis=-1, keepdims=True))
                p = jnp.exp(s - m_new)
                acc = jnp.exp(m - m_new) * acc + jnp.dot(
                    p.astype(jnp.bfloat16), jnp.where(hm, vj, jnp.ones_like(vj)),
                    preferred_element_type=jnp.float32)
                return m_new, acc

            m, acc = lax.fori_loop(0, i, past_block, (m, acc))
            res.append(acc / pltpu.roll(acc, HEAD_DIM, 1))
        o_ref[0, pl.ds(row0, blk), :] = jnp.where(lo, res[0], res[1]).astype(o_ref.dtype)
        return carry

    lax.fori_loop(0, nb, q_block, 0)


def _moba_call(qb, kb, vb, bias_mb):
    B, S, _ = qb.shape
    nb = S // MOBA_BLOCK
    pairs = MOBA_HEADS // 2
    slab = pl.BlockSpec((1, S, LANES), lambda hp, b: (b, 0, hp))
    return pl.pallas_call(
        _moba_kernel,
        grid=(pairs, B),
        in_specs=[slab, slab, slab,
                  pl.BlockSpec((2, nb, MOBA_BLOCK, MOBA_BLOCK), lambda hp, b: (hp, 0, 0, 0))],
        out_specs=slab,
        out_shape=jax.ShapeDtypeStruct((B, S, MOBA_WIDTH), jnp.bfloat16),
        compiler_params=pltpu.CompilerParams(vmem_limit_bytes=VMEM_LIMIT),
        name="moba",
    )(qb, kb, vb, bias_mb)


def _outproj_kernel(ya_ref, yb_ref, x_ref, ga_ref, gb_ref, w_ref, g1_ref, n2_ref, sh_ref, sc_ref, wr_ref,
                    x1_ref, h2t_ref, h2b_ref, lg_ref):
    def norm(y, g):
        y = y.astype(jnp.float32)
        return y * lax.rsqrt(jnp.mean(y * y, axis=-1, keepdims=True) + EPS) * g

    yn = jnp.concatenate([norm(ya_ref[...], ga_ref[...]), norm(yb_ref[...], gb_ref[...])], axis=-1)
    y = jnp.dot(yn.astype(jnp.bfloat16), w_ref[...], preferred_element_type=jnp.float32)
    x1 = x_ref[...] + g1_ref[0] * y
    x1_ref[...] = x1
    h2 = x1 * lax.rsqrt(jnp.mean(x1 * x1, axis=-1, keepdims=True) + EPS) * n2_ref[...]
    h2 = h2 * (1.0 + sc_ref[0]) + sh_ref[0]
    for j in range(ROW_CHUNKS):
        h2t_ref[:, j, :] = h2[:, j * LANES:(j + 1) * LANES]
    h2b_ref[...] = h2.astype(jnp.bfloat16)
    lg_ref[...] = lax.dot_general(wr_ref[...], h2, _NT, precision=_HIGHEST,
                                  preferred_element_type=jnp.float32)


def _outproj_call(ya, yb, x2, ga, gb, w_bf16, g1, n2, sh2, sc2, w_router_t, seq):
    n_tok, D = x2.shape
    tm = OUTPROJ_TM
    per_seq = seq // tm
    row = lambda i: (i, 0)
    const = lambda i: (0, 0)
    mod_spec = pl.BlockSpec((1, 1, D), lambda i: (i // per_seq, 0, 0))
    return pl.pallas_call(
        _outproj_kernel,
        grid=(n_tok // tm,),
        in_specs=[pl.BlockSpec((tm, SWA_WIDTH), row), pl.BlockSpec((tm, MOBA_WIDTH), row),
                  pl.BlockSpec((tm, D), row),
                  pl.BlockSpec((1, SWA_WIDTH), const), pl.BlockSpec((1, MOBA_WIDTH), const),
                  pl.BlockSpec((MIX_WIDTH, D), const),
                  mod_spec, pl.BlockSpec((1, D), const), mod_spec, mod_spec,
                  pl.BlockSpec((N_EXPERTS, D), const)],
        out_specs=[pl.BlockSpec((tm, D), row),
                   pl.BlockSpec((tm, ROW_CHUNKS, LANES), lambda i: (i, 0, 0)),
                   pl.BlockSpec((tm, D), row),
                   pl.BlockSpec((N_EXPERTS, tm), lambda i: (0, i))],
        out_shape=[jax.ShapeDtypeStruct((n_tok, D), jnp.float32),
                   jax.ShapeDtypeStruct((n_tok, ROW_CHUNKS, LANES), jnp.float32),
                   jax.ShapeDtypeStruct((n_tok, D), jnp.bfloat16),
                   jax.ShapeDtypeStruct((N_EXPERTS, n_tok), jnp.float32)],
        compiler_params=pltpu.CompilerParams(vmem_limit_bytes=VMEM_LIMIT),
        name="outproj",
    )(ya, yb, x2, ga, gb, w_bf16, g1, n2, sh2, sc2, w_router_t)


def _route_kernel(lg_ref, eb_ref, eidx_ref, w_ref):
    tn = lg_ref.shape[1]
    scores = 1.0 / (1.0 + jnp.exp(-lg_ref[...]))
    sel = scores + eb_ref[...]
    neg = -jnp.inf
    g_iota = lax.broadcasted_iota(jnp.int32, (GROUP_SIZE, tn), 0)
    gs = []
    for g in range(N_GROUPS):
        blk = sel[g * GROUP_SIZE:(g + 1) * GROUP_SIZE, :]
        m1 = jnp.max(blk, axis=0, keepdims=True)
        i1 = jnp.min(jnp.where(blk == m1, g_iota, GROUP_SIZE), axis=0, keepdims=True)
        m2 = jnp.max(jnp.where(g_iota == i1, neg, blk), axis=0, keepdims=True)
        gs.append(m1 + m2)
    masked = []
    for g in range(N_GROUPS):
        rank = jnp.zeros((1, tn), jnp.float32)
        for gp in range(N_GROUPS):
            if gp == g:
                continue
            beats = (gs[gp] > gs[g]) | ((gs[gp] == gs[g]) & (gp < g))
            rank = rank + jnp.where(beats, 1.0, 0.0)
        keep = rank < TOPK_GROUPS
        masked.append(jnp.where(keep, sel[g * GROUP_SIZE:(g + 1) * GROUP_SIZE, :], neg))
    masked = jnp.concatenate(masked, axis=0)
    e_iota = lax.broadcasted_iota(jnp.int32, (N_EXPERTS, tn), 0)
    idxs, ws = [], []
    for _ in range(TOP_K):
        m = jnp.max(masked, axis=0, keepdims=True)
        idx = jnp.min(jnp.where(masked == m, e_iota, N_EXPERTS), axis=0, keepdims=True)
        hit = e_iota == idx
        ws.append(jnp.sum(jnp.where(hit, scores, 0.0), axis=0, keepdims=True))
        masked = jnp.where(hit, neg, masked)
        idxs.append(idx)
    wsum = ws[0]
    for k in range(1, TOP_K):
        wsum = wsum + ws[k]
    eidx_ref[...] = jnp.concatenate(idxs, axis=0)
    w_ref[...] = jnp.concatenate(ws, axis=0) / wsum * ROUTED_SCALE


def _route_call(lg_t, e_bias):
    n_tok = lg_t.shape[1]
    tn = ROUTE_TN
    return pl.pallas_call(
        _route_kernel,
        grid=(n_tok // tn,),
        in_specs=[pl.BlockSpec((N_EXPERTS, tn), lambda i: (0, i)),
                  pl.BlockSpec((N_EXPERTS, 1), lambda i: (0, 0))],
        out_specs=[pl.BlockSpec((TOP_K, tn), lambda i: (0, i)),
                   pl.BlockSpec((TOP_K, tn), lambda i: (0, i))],
        out_shape=[jax.ShapeDtypeStruct((TOP_K, n_tok), jnp.int32),
                   jax.ShapeDtypeStruct((TOP_K, n_tok), jnp.float32)],
        compiler_params=pltpu.CompilerParams(vmem_limit_bytes=VMEM_LIMIT),
        name="route",
    )(lg_t, e_bias.reshape(N_EXPERTS, 1))


def _moe_row_copy(h2_hbm, xg, sem, tok, slot, r):
    return pltpu.make_async_copy(h2_hbm.at[tok], xg.at[slot, r], sem.at[slot])


def _moe_kernel(blk_e_ref, nvalid_ref, tokc_ref, tokn_ref, h2_hbm, w1_ref, w3_ref, w2_ref, ys_ref, xg, sem):
    del blk_e_ref
    i = pl.program_id(0)
    nvalid = nvalid_ref[0]
    slot = i % 2

    def issue(tok_ref, s):
        for r in range(MOE_TM):
            _moe_row_copy(h2_hbm, xg, sem, tok_ref[0, 0, r], s, r).start()

    @pl.when((i == 0) & (nvalid > 0))
    def _():
        issue(tokc_ref, 0)

    @pl.when(i + 1 < nvalid)
    def _():
        issue(tokn_ref, 1 - slot)

    @pl.when(i < nvalid)
    def _():
        for r in range(MOE_TM):
            _moe_row_copy(h2_hbm, xg, sem, 0, slot, r).wait()
        x = jnp.concatenate([xg[slot, :, j, :] for j in range(ROW_CHUNKS)], axis=-1).astype(jnp.bfloat16)
        a = jnp.dot(x, w1_ref[0].astype(jnp.bfloat16), preferred_element_type=jnp.float32)
        b = jnp.dot(x, w3_ref[0].astype(jnp.bfloat16), preferred_element_type=jnp.float32)
        h = (_silu(a) * b).astype(jnp.bfloat16)
        y = jnp.dot(h, w2_ref[0].astype(jnp.bfloat16), preferred_element_type=jnp.float32)
        for j in range(ROW_CHUNKS):
            ys_ref[:, j, :] = y[:, j * LANES:(j + 1) * LANES]

    @pl.when(i >= nvalid)
    def _():
        ys_ref[...] = jnp.zeros_like(ys_ref)


def _moe_call(blk_e, nvalid, buf_tok3, h2t, w1, w3, w2):
    nblk = buf_tok3.shape[0]
    D = D_MODEL
    grid_spec = pltpu.PrefetchScalarGridSpec(
        num_scalar_prefetch=2,
        grid=(nblk,),
        in_specs=[pl.BlockSpec((1, 1, MOE_TM), lambda i, be, nv: (i, 0, 0), memory_space=pltpu.SMEM),
                  pl.BlockSpec((1, 1, MOE_TM), lambda i, be, nv: (jnp.minimum(i + 1, nblk - 1), 0, 0),
                               memory_space=pltpu.SMEM),
                  pl.BlockSpec(memory_space=pl.ANY),
                  pl.BlockSpec((1, D, EXPERT_DIM), lambda i, be, nv: (be[i], 0, 0)),
                  pl.BlockSpec((1, D, EXPERT_DIM), lambda i, be, nv: (be[i], 0, 0)),
                  pl.BlockSpec((1, EXPERT_DIM, D), lambda i, be, nv: (be[i], 0, 0))],
        out_specs=pl.BlockSpec((MOE_TM, ROW_CHUNKS, LANES), lambda i, be, nv: (i, 0, 0)),
        scratch_shapes=[pltpu.VMEM((2, MOE_TM, ROW_CHUNKS, LANES), jnp.float32),
                        pltpu.SemaphoreType.DMA((2,))],
    )
    return pl.pallas_call(
        _moe_kernel,
        grid_spec=grid_spec,
        out_shape=jax.ShapeDtypeStruct((nblk * MOE_TM, ROW_CHUNKS, LANES), jnp.float32),
        compiler_params=pltpu.CompilerParams(vmem_limit_bytes=VMEM_LIMIT,
                                             dimension_semantics=("arbitrary",)),
        name="moe",
    )(blk_e, nvalid, buf_tok3, buf_tok3, h2t, w1, w3, w2)


def _combine_row_copy(ys_hbm, buf, sem, row, k, t):
    return pltpu.make_async_copy(ys_hbm.at[row], buf.at[k, t], sem.at[0])


def _combine_kernel(dest_ref, ys_hbm, wk_ref, x1_ref, h2_ref, g2_ref, ws1_ref, ws3_ref, ws2_ref, fg_ref,
                    o_ref, buf, sem):
    tm = COMBINE_TM
    for k in range(TOP_K):
        for t in range(tm):
            _combine_row_copy(ys_hbm, buf, sem, dest_ref[0, 0, k * tm + t], k, t).start()
    h2 = h2_ref[...]
    a = jnp.dot(h2, ws1_ref[...], preferred_element_type=jnp.float32)
    b = jnp.dot(h2, ws3_ref[...], preferred_element_type=jnp.float32)
    acc = jnp.dot((_silu(a) * b).astype(jnp.bfloat16), ws2_ref[...], preferred_element_type=jnp.float32)
    for k in range(TOP_K):
        for t in range(tm):
            _combine_row_copy(ys_hbm, buf, sem, 0, k, t).wait()
    wk = wk_ref[...]
    for k in range(TOP_K):
        rows = jnp.concatenate([buf[k, :, j, :] for j in range(ROW_CHUNKS)], axis=-1)
        acc = acc + wk[:, k:k + 1] * rows
    x = x1_ref[...] + g2_ref[0] * acc
    o_ref[...] = x * lax.rsqrt(jnp.mean(x * x, axis=-1, keepdims=True) + EPS) * fg_ref[...]


def _combine_call(dest3, ys, wk, x1, h2b, g2, ws1, ws3, ws2, fg, seq):
    n_tok, D = x1.shape
    tm = COMBINE_TM
    per_seq = seq // tm
    row = lambda i: (i, 0)
    const = lambda i: (0, 0)
    return pl.pallas_call(
        _combine_kernel,
        grid=(n_tok // tm,),
        in_specs=[pl.BlockSpec((1, 1, TOP_K * tm), lambda i: (i, 0, 0), memory_space=pltpu.SMEM),
                  pl.BlockSpec(memory_space=pl.ANY),
                  pl.BlockSpec((tm, TOP_K), row),
                  pl.BlockSpec((tm, D), row),
                  pl.BlockSpec((tm, D), row),
                  pl.BlockSpec((1, 1, D), lambda i: (i // per_seq, 0, 0)),
                  pl.BlockSpec((D, SHARED_DIM), const),
                  pl.BlockSpec((D, SHARED_DIM), const),
                  pl.BlockSpec((SHARED_DIM, D), const),
                  pl.BlockSpec((1, D), const)],
        out_specs=pl.BlockSpec((tm, D), row),
        out_shape=jax.ShapeDtypeStruct((n_tok, D), jnp.float32),
        scratch_shapes=[pltpu.VMEM((TOP_K, tm, ROW_CHUNKS, LANES), jnp.float32),
                        pltpu.SemaphoreType.DMA((1,))],
        compiler_params=pltpu.CompilerParams(vmem_limit_bytes=VMEM_LIMIT),
        name="combine",
    )(dest3, ys, wk, x1, h2b, g2, ws1, ws3, ws2, fg)


def _swa_head_perm():
    cols = []
    for c in range(SWA_GROUP):
        for kvh in range(SWA_KV_HEADS):
            h = kvh * SWA_GROUP + c
            cols.extend(range(h * HEAD_DIM, (h + 1) * HEAD_DIM))
    return jnp.asarray(cols, jnp.int32)


def _bias_tables(rel_table, seq):
    vec = rel_table.astype(jnp.float32)[_t5_bucket(jnp.arange(seq, dtype=jnp.int32))]
    qi = jnp.arange(SWA_BLOCK)[:, None]
    kj = jnp.arange(2 * SWA_BLOCK)[None, :]
    d_a = jnp.clip(qi + SWA_BLOCK - kj, 0, seq - 1)
    bias_a = jnp.transpose(vec[d_a][:, :, :SWA_Q_HEADS], (2, 0, 1))
    bias_sw = bias_a.reshape(SWA_KV_HEADS, SWA_GROUP * SWA_BLOCK, 2 * SWA_BLOCK)
    nb = seq // MOBA_BLOCK
    r = jnp.arange(MOBA_BLOCK)
    d_b = jnp.clip(jnp.arange(nb)[:, None, None] * MOBA_BLOCK + r[None, :, None] - r[None, None, :], 0, seq - 1)
    bias_mb = jnp.transpose(vec[d_b][..., SWA_Q_HEADS:], (3, 0, 1, 2))
    return bias_sw, bias_mb


def _route_plan(eidx_t, n_tok):
    A = n_tok * TOP_K
    flat_e = eidx_t.T.reshape(A)
    order = jnp.argsort(flat_e)
    se = flat_e[order]
    stok = (order // TOP_K).astype(jnp.int32)
    counts = jnp.zeros((N_EXPERTS,), jnp.int32).at[flat_e].add(1)
    starts = jnp.cumsum(counts) - counts
    pcounts = (counts + MOE_TM - 1) // MOE_TM * MOE_TM
    pends = jnp.cumsum(pcounts)
    pstarts = pends - pcounts
    dest_sorted = (pstarts[se] + (jnp.arange(A, dtype=jnp.int32) - starts[se])).astype(jnp.int32)
    P = A + N_EXPERTS * MOE_TM
    nblk = P // MOE_TM
    buf_tok = jnp.zeros((P,), jnp.int32).at[dest_sorted].set(stok)
    dest = jnp.zeros((A,), jnp.int32).at[order].set(dest_sorted).reshape(n_tok, TOP_K)
    blk_e = jnp.minimum(jnp.searchsorted(pends, jnp.arange(nblk, dtype=jnp.int32) * MOE_TM, side='right'),
                        N_EXPERTS - 1).astype(jnp.int32)
    nvalid = (pends[-1] // MOE_TM).astype(jnp.int32).reshape(1)
    return buf_tok.reshape(nblk, 1, MOE_TM), blk_e, nvalid, dest


def kernel(x, c, w_ada, b_ada, norm1_g, w_in, sinks, rel_table, out_norm_a, out_norm_b, w_out, norm2_g,
           w_router, e_bias, w1, w3, w2, ws1, ws3, ws2, final_g):
    B, S, D = x.shape
    assert w_ada.shape[0] == 1, "single-layer stack only"
    assert D == D_MODEL and S % INPROJ_TM == 0 and S % MOBA_BLOCK == 0
    n_tok = B * S
    bf16 = jnp.bfloat16
    perm = _swa_head_perm()
    in_perm = jnp.concatenate([perm, jnp.arange(SWA_WIDTH, IN_COLS, dtype=jnp.int32)])
    out_perm = jnp.concatenate([perm, jnp.arange(SWA_WIDTH, MIX_WIDTH, dtype=jnp.int32)])
    bias_sw, bias_mb = _bias_tables(rel_table, S)
    x2 = x.reshape(n_tok, D)
    l = 0
    mod = _ada_call(c, w_ada[l], b_ada[l])
    sh1, sc1, g1, sh2, sc2, g2 = [m.reshape(B, 1, D) for m in jnp.split(mod, 6, axis=-1)]
    qa, ka, va, qb, kb, vb = _inproj_call(x2, norm1_g[l].reshape(1, D), sh1, sc1,
                                          w_in[l][:, in_perm].astype(bf16), S)
    sink_col = jnp.repeat(sinks[l].astype(jnp.float32), SWA_BLOCK).reshape(
        SWA_KV_HEADS, SWA_GROUP * SWA_BLOCK, 1)
    ya = _swa_call(qa.reshape(B, S, SWA_WIDTH), ka.reshape(B, S, SWA_KV_WIDTH),
                   va.reshape(B, S, SWA_KV_WIDTH), bias_sw, sink_col)
    yb = _moba_call(qb.reshape(B, S, MOBA_WIDTH), kb.reshape(B, S, MOBA_WIDTH),
                    vb.reshape(B, S, MOBA_WIDTH), bias_mb)
    x1, h2t, h2b, lg_t = _outproj_call(
        ya.reshape(n_tok, SWA_WIDTH), yb.reshape(n_tok, MOBA_WIDTH), x2,
        out_norm_a[l][perm].reshape(1, SWA_WIDTH), out_norm_b[l].reshape(1, MOBA_WIDTH),
        w_out[l][out_perm, :].astype(bf16), g1, norm2_g[l].reshape(1, D), sh2, sc2,
        w_router[l].T, S)
    eidx_t, w_t = _route_call(lg_t, e_bias[l])
    buf_tok3, blk_e, nvalid, dest = _route_plan(eidx_t, n_tok)
    ys = _moe_call(blk_e, nvalid, buf_tok3, h2t, w1[l], w3[l], w2[l])
    dest3 = jnp.transpose(dest.reshape(n_tok // COMBINE_TM, COMBINE_TM, TOP_K), (0, 2, 1)).reshape(
        n_tok // COMBINE_TM, 1, TOP_K * COMBINE_TM)
    out = _combine_call(dest3, ys, w_t.T, x1, h2b, g2, ws1[l].astype(bf16), ws3[l].astype(bf16),
                        ws2[l].astype(bf16), final_g.reshape(1, D), S)
    return out.reshape(B, S, D)
```

```python
import math

import jax
import jax.numpy as jnp
from jax import lax
from jax.experimental import pallas as pl
from jax.experimental.pallas import tpu as pltpu

D_MODEL = 1024
HEAD_DIM = 64
SWA_Q_HEADS = 8
SWA_KV_HEADS = 2
SWA_GROUP = SWA_Q_HEADS // SWA_KV_HEADS
SWA_WINDOW = 128
SWA_BLOCK = 128
SWA_WIDTH = SWA_Q_HEADS * HEAD_DIM
SWA_KV_WIDTH = SWA_KV_HEADS * HEAD_DIM
MOBA_HEADS = 8
MOBA_BLOCK = 256
MOBA_TOPK = 3
MOBA_WIDTH = MOBA_HEADS * HEAD_DIM
N_HEADS = SWA_Q_HEADS + MOBA_HEADS
MIX_WIDTH = SWA_WIDTH + MOBA_WIDTH
IN_COLS = SWA_WIDTH + 2 * SWA_KV_WIDTH + 3 * MOBA_WIDTH
ATTN_SCALE = HEAD_DIM ** -0.5
REL_BUCKETS = 32
REL_MAX_DIST = 1024
N_EXPERTS = 256
TOP_K = 8
N_GROUPS = 8
TOPK_GROUPS = 4
GROUP_SIZE = N_EXPERTS // N_GROUPS
EXPERT_DIM = 256
SHARED_DIM = 256
ROUTED_SCALE = 2.5
EPS = 1e-6
NEG_INF = -1e30

LANES = 128
SUBLANES = 8
ROW_CHUNKS = D_MODEL // LANES
HALF = D_MODEL // 2
PACK_CHUNKS = HALF // LANES

ADA_TN = 512
INPROJ_TM = 512
OUTPROJ_TM = 256
ROUTE_TN = 512
MOE_TM = 256
MOE_XT_STRIDE = MOE_TM + SUBLANES
COMBINE_TM = 128
KEY_SHIFT = 18
VMEM_LIMIT = 48 * 1024 * 1024
MOE_VMEM_LIMIT = 58 * 1024 * 1024

_HIGHEST = lax.Precision.HIGHEST
_NT = (((1,), (1,)), ((), ()))
_TN = (((0,), (0,)), ((), ()))
_HI_MASK = -65536


def _silu(a):
    return a * (1.0 / (1.0 + jnp.exp(-a)))


def _t5_bucket(dist):
    n = jnp.maximum(dist, 0)
    max_exact = REL_BUCKETS // 2
    nf = jnp.maximum(n, 1).astype(jnp.float32)
    large = max_exact + (jnp.log(nf / max_exact) / math.log(REL_MAX_DIST / max_exact)
                         * (REL_BUCKETS - max_exact)).astype(jnp.int32)
    large = jnp.minimum(large, REL_BUCKETS - 1)
    return jnp.where(n < max_exact, n, large)


def _ada_kernel(c_ref, w_ref, b_ref, o_ref):
    c = c_ref[...]
    o_ref[...] = jnp.dot(_silu(c), w_ref[...], precision=_HIGHEST,
                         preferred_element_type=jnp.float32) + b_ref[...]


def _ada_call(c, w_ada, b_ada):
    B, D = c.shape
    n_out = w_ada.shape[1]
    return pl.pallas_call(
        _ada_kernel,
        grid=(n_out // ADA_TN,),
        in_specs=[pl.BlockSpec((B, D), lambda j: (0, 0)),
                  pl.BlockSpec((D, ADA_TN), lambda j: (0, j)),
                  pl.BlockSpec((1, ADA_TN), lambda j: (0, j))],
        out_specs=pl.BlockSpec((B, ADA_TN), lambda j: (0, j)),
        out_shape=jax.ShapeDtypeStruct((B, n_out), jnp.float32),
        name="adaln",
    )(c, w_ada, b_ada.reshape(1, n_out))


def _inproj_kernel(x_ref, g_ref, sh_ref, sc_ref, w_ref, qa_ref, ka_ref, va_ref, qb_ref, kb_ref, vb_ref):
    x = x_ref[...]
    h = x * lax.rsqrt(jnp.mean(x * x, axis=-1, keepdims=True) + EPS) * g_ref[...]
    h = h * (1.0 + sc_ref[0]) + sh_ref[0]
    p = jnp.dot(h.astype(jnp.bfloat16), w_ref[...], preferred_element_type=jnp.float32)
    off = 0
    for ref in (qa_ref, ka_ref, va_ref, qb_ref, kb_ref, vb_ref):
        width = ref.shape[-1]
        ref[...] = p[:, off:off + width].astype(ref.dtype)
        off += width


def _inproj_call(x2, g, sh, sc, w_bf16, seq):
    n_tok, D = x2.shape
    tm = INPROJ_TM
    per_seq = seq // tm
    widths = (SWA_WIDTH, SWA_KV_WIDTH, SWA_KV_WIDTH, MOBA_WIDTH, MOBA_WIDTH, MOBA_WIDTH)
    mod_spec = pl.BlockSpec((1, 1, D), lambda i: (i // per_seq, 0, 0))
    return pl.pallas_call(
        _inproj_kernel,
        grid=(n_tok // tm,),
        in_specs=[pl.BlockSpec((tm, D), lambda i: (i, 0)),
                  pl.BlockSpec((1, D), lambda i: (0, 0)),
                  mod_spec, mod_spec,
                  pl.BlockSpec((D, IN_COLS), lambda i: (0, 0))],
        out_specs=[pl.BlockSpec((tm, w), lambda i: (i, 0)) for w in widths],
        out_shape=[jax.ShapeDtypeStruct((n_tok, w), jnp.bfloat16) for w in widths],
        compiler_params=pltpu.CompilerParams(vmem_limit_bytes=VMEM_LIMIT),
        name="inproj",
    )(x2, g, sh, sc, w_bf16)


def _swa_kernel(q_ref, kc_ref, kp_ref, vc_ref, vp_ref, tab_ref, sink_ref, o_ref, bias_ref):
    n = pl.program_id(1)
    rows = SWA_GROUP * SWA_BLOCK
    lo = lax.broadcasted_iota(jnp.int32, (1, LANES), 1) < HEAD_DIM

    @pl.when((pl.program_id(0) == 0) & (n == 0))
    def _():
        for h in range(SWA_Q_HEADS):
            kvh, c = divmod(h, SWA_GROUP)
            row = jnp.broadcast_to(tab_ref[h:h + 1, :], (SWA_BLOCK, 2 * SWA_BLOCK))
            bias_ref[kvh, c * SWA_BLOCK:(c + 1) * SWA_BLOCK, :] = pltpu.roll(row, 0, 1, stride=1, stride_axis=0)

    kk = jnp.concatenate([kp_ref[0], kc_ref[0]], axis=0)
    vv = jnp.concatenate([vp_ref[0], vc_ref[0]], axis=0)
    qi = lax.broadcasted_iota(jnp.int32, (rows, 2 * SWA_BLOCK), 0) & (SWA_BLOCK - 1)
    kj = lax.broadcasted_iota(jnp.int32, (rows, 2 * SWA_BLOCK), 1)
    dist = qi + SWA_BLOCK - kj
    valid = (dist >= 0) & (dist < SWA_WINDOW) & ((kj >= SWA_BLOCK) | (n > 0))
    qs = jnp.concatenate([q_ref[0, :, c * LANES:(c + 1) * LANES] for c in range(SWA_GROUP)], axis=0)
    outs = []
    for kvh in range(SWA_KV_HEADS):
        hm = lo if kvh == 0 else jnp.logical_not(lo)
        qm = jnp.where(hm, qs, jnp.zeros_like(qs))
        s = lax.dot_general(qm, kk, _NT, preferred_element_type=jnp.float32)
        s = s * ATTN_SCALE + bias_ref[kvh]
        s = jnp.where(valid, s, NEG_INF)
        sink = sink_ref[kvh]
        m = jnp.maximum(jnp.max(s, axis=-1, keepdims=True), sink)
        p = jnp.exp(s - m)
        vm = jnp.where(hm, vv, jnp.ones_like(vv))
        acc = jnp.dot(p.astype(jnp.bfloat16), vm, preferred_element_type=jnp.float32)
        denom = pltpu.roll(acc, HEAD_DIM, 1) + jnp.exp(sink - m)
        outs.append(acc / denom)
    for c in range(SWA_GROUP):
        blk = jnp.where(lo, outs[0][c * SWA_BLOCK:(c + 1) * SWA_BLOCK],
                        outs[1][c * SWA_BLOCK:(c + 1) * SWA_BLOCK])
        o_ref[0, :, c * LANES:(c + 1) * LANES] = blk.astype(o_ref.dtype)


def _swa_call(qa, ka, va, tab_sw, sink_col):
    B, S, _ = qa.shape
    nb = S // SWA_BLOCK
    rows = SWA_GROUP * SWA_BLOCK
    cur = lambda b, n: (b, n, 0)
    prev = lambda b, n: (b, jnp.maximum(n - 1, 0), 0)
    kv_blk = (1, SWA_BLOCK, SWA_KV_WIDTH)
    return pl.pallas_call(
        _swa_kernel,
        grid=(B, nb),
        in_specs=[pl.BlockSpec((1, SWA_BLOCK, SWA_WIDTH), cur),
                  pl.BlockSpec(kv_blk, cur), pl.BlockSpec(kv_blk, prev),
                  pl.BlockSpec(kv_blk, cur), pl.BlockSpec(kv_blk, prev),
                  pl.BlockSpec((SWA_Q_HEADS, 2 * SWA_BLOCK), lambda b, n: (0, 0)),
                  pl.BlockSpec((SWA_KV_HEADS, rows, 1), lambda b, n: (0, 0, 0))],
        out_specs=pl.BlockSpec((1, SWA_BLOCK, SWA_WIDTH), cur),
        out_shape=jax.ShapeDtypeStruct((B, S, SWA_WIDTH), jnp.bfloat16),
        scratch_shapes=[pltpu.VMEM((SWA_KV_HEADS, rows, 2 * SWA_BLOCK), jnp.float32)],
        compiler_params=pltpu.CompilerParams(vmem_limit_bytes=VMEM_LIMIT,
                                             dimension_semantics=("arbitrary", "arbitrary")),
        name="swa",
    )(qa, ka, ka, va, va, tab_sw, sink_col)


def _moba_kernel(q_ref, k_ref, v_ref, tab_ref, o_ref, bias_ref):
    S = q_ref.shape[1]
    nb = S // MOBA_BLOCK
    blk = MOBA_BLOCK
    lane_lo = lax.broadcasted_iota(jnp.int32, (1, LANES), 1) < HEAD_DIM
    head_masks = (lane_lo, jnp.logical_not(lane_lo))

    @pl.when(pl.program_id(1) == 0)
    def _():
        for hh in range(2):
            for d in range(nb):
                row = jnp.broadcast_to(tab_ref[hh, d:d + 1, :], (blk, 2 * blk))
                bias_ref[hh, d] = pltpu.roll(row, blk, 1, stride=1, stride_axis=0)[:, :blk]

    kmeans = [jnp.mean(k_ref[0, j * blk:(j + 1) * blk, :].astype(jnp.float32), axis=0, keepdims=True)
              for j in range(nb)]
    kmean = jnp.concatenate(kmeans + [jnp.zeros((LANES - nb, LANES), jnp.float32)], axis=0)
    causal_t = (lax.broadcasted_iota(jnp.int32, (blk, blk), 0)
                <= lax.broadcasted_iota(jnp.int32, (blk, blk), 1))
    blk_id = lax.broadcasted_iota(jnp.int32, (nb, blk), 0)

    def masked_v(v, hh):
        return jnp.where(head_masks[hh], v, jnp.ones_like(v))

    def q_block(i, carry):
        row0 = pl.multiple_of(i * blk, blk)
        q = q_ref[0, pl.ds(row0, blk), :]
        kd = k_ref[0, pl.ds(row0, blk), :]
        vd = v_ref[0, pl.ds(row0, blk), :]
        qms, sels, state = [], [], []
        for hh in range(2):
            qm = jnp.where(head_masks[hh], q, jnp.zeros_like(q))
            gate = lax.dot_general(kmean, qm.astype(jnp.float32), _NT, precision=_HIGHEST,
                                   preferred_element_type=jnp.float32)[:nb, :]
            rank = jnp.zeros((nb, blk), jnp.float32)
            for jp in range(nb):
                gj = gate[jp:jp + 1, :]
                beats = jnp.where(gj > gate, 1.0, jnp.where((gj == gate) & (jp < blk_id), 1.0, 0.0))
                rank = rank + jnp.where(jp < i, beats, 0.0)
            sel = jnp.where(rank < MOBA_TOPK, 1.0, 0.0)
            s = lax.dot_general(kd, qm, _NT, preferred_element_type=jnp.float32)
            s = s * ATTN_SCALE + bias_ref[hh, 0]
            s = jnp.where(causal_t, s, NEG_INF)
            m = jnp.max(s, axis=0, keepdims=True)
            p = jnp.exp(s - m)
            acc = lax.dot_general(masked_v(vd, hh), p.astype(jnp.bfloat16), _TN,
                                  preferred_element_type=jnp.float32)
            qms.append(qm)
            sels.append(sel)
            state.extend([m, acc])

        def past_block(j, st):
            r0 = pl.multiple_of(j * blk, blk)
            kj = k_ref[0, pl.ds(r0, blk), :]
            vj = v_ref[0, pl.ds(r0, blk), :]
            new = []
            for hh in range(2):
                m, acc = st[2 * hh], st[2 * hh + 1]
                s = lax.dot_general(kj, qms[hh], _NT, preferred_element_type=jnp.float32)
                s = s * ATTN_SCALE + bias_ref[hh, i - j]
                picked = jnp.sum(jnp.where(blk_id == j, sels[hh], 0.0), axis=0, keepdims=True) > 0.5
                s = jnp.where(picked, s, NEG_INF)
                m_new = jnp.maximum(m, jnp.max(s, axis=0, keepdims=True))
                p = jnp.exp(s - m_new)
                acc = jnp.exp(m - m_new) * acc + lax.dot_general(
                    masked_v(vj, hh), p.astype(jnp.bfloat16), _TN, preferred_element_type=jnp.float32)
                new.extend([m_new, acc])
            return tuple(new)

        st = lax.fori_loop(0, i, past_block, tuple(state))
        o0 = st[1][:HEAD_DIM] / st[1][HEAD_DIM:]
        o1 = st[3][HEAD_DIM:] / st[3][:HEAD_DIM]
        out_t = jnp.concatenate([o0, o1], axis=0)
        o_ref[0, pl.ds(row0, blk), :] = out_t.T.astype(o_ref.dtype)
        return carry

    lax.fori_loop(0, nb, q_block, 0)


def _moba_call(qb, kb, vb, tab_mb):
    B, S, _ = qb.shape
    nb = S // MOBA_BLOCK
    pairs = MOBA_HEADS // 2
    slab = pl.BlockSpec((1, S, LANES), lambda hp, b: (b, 0, hp))
    return pl.pallas_call(
        _moba_kernel,
        grid=(pairs, B),
        in_specs=[slab, slab, slab,
                  pl.BlockSpec((2, nb, 2 * MOBA_BLOCK), lambda hp, b: (hp, 0, 0))],
        out_specs=slab,
        out_shape=jax.ShapeDtypeStruct((B, S, MOBA_WIDTH), jnp.bfloat16),
        scratch_shapes=[pltpu.VMEM((2, nb, MOBA_BLOCK, MOBA_BLOCK), jnp.float32)],
        compiler_params=pltpu.CompilerParams(vmem_limit_bytes=VMEM_LIMIT,
                                             dimension_semantics=("arbitrary", "arbitrary")),
        name="moba",
    )(qb, kb, vb, tab_mb)


def _outproj_kernel(ya_ref, yb_ref, x_ref, ga_ref, gb_ref, w_ref, g1_ref, n2_ref, sh_ref, sc_ref, wr_ref,
                    x1_ref, h2p_ref, h2b_ref, lg_ref):
    tm = x_ref.shape[0]

    def norm(y, g):
        y = y.astype(jnp.float32)
        return y * lax.rsqrt(jnp.mean(y * y, axis=-1, keepdims=True) + EPS) * g

    yn = jnp.concatenate([norm(ya_ref[...], ga_ref[...]), norm(yb_ref[...], gb_ref[...])], axis=-1)
    y = jnp.dot(yn.astype(jnp.bfloat16), w_ref[...], preferred_element_type=jnp.float32)
    x1 = x_ref[...] + g1_ref[0] * y
    x1_ref[...] = x1
    h2 = x1 * lax.rsqrt(jnp.mean(x1 * x1, axis=-1, keepdims=True) + EPS) * n2_ref[...]
    h2 = h2 * (1.0 + sc_ref[0]) + sh_ref[0]
    h2b = h2.astype(jnp.bfloat16)
    h2b_ref[...] = h2b
    bits = lax.bitcast_convert_type(h2b.astype(jnp.float32), jnp.int32)
    word = (bits[:, HALF:] & _HI_MASK) | lax.shift_right_logical(bits[:, :HALF], 16)
    for j in range(PACK_CHUNKS):
        h2p_ref[pl.ds(j, tm, stride=PACK_CHUNKS), :] = word[:, j * LANES:(j + 1) * LANES]
    lg_ref[...] = lax.dot_general(wr_ref[...], h2, _NT, precision=_HIGHEST,
                                  preferred_element_type=jnp.float32)


def _outproj_call(ya, yb, x2, ga, gb, w_bf16, g1, n2, sh2, sc2, w_router_t, seq):
    n_tok, D = x2.shape
    tm = OUTPROJ_TM
    per_seq = seq // tm
    row = lambda i: (i, 0)
    const = lambda i: (0, 0)
    mod_spec = pl.BlockSpec((1, 1, D), lambda i: (i // per_seq, 0, 0))
    return pl.pallas_call(
        _outproj_kernel,
        grid=(n_tok // tm,),
        in_specs=[pl.BlockSpec((tm, SWA_WIDTH), row), pl.BlockSpec((tm, MOBA_WIDTH), row),
                  pl.BlockSpec((tm, D), row),
                  pl.BlockSpec((1, SWA_WIDTH), const), pl.BlockSpec((1, MOBA_WIDTH), const),
                  pl.BlockSpec((MIX_WIDTH, D), const),
                  mod_spec, pl.BlockSpec((1, D), const), mod_spec, mod_spec,
                  pl.BlockSpec((N_EXPERTS, D), const)],
        out_specs=[pl.BlockSpec((tm, D), row),
                   pl.BlockSpec((tm * PACK_CHUNKS, LANES), row),
                   pl.BlockSpec((tm, D), row),
                   pl.BlockSpec((N_EXPERTS, tm), lambda i: (0, i))],
        out_shape=[jax.ShapeDtypeStruct((n_tok, D), jnp.float32),
                   jax.ShapeDtypeStruct((n_tok * PACK_CHUNKS, LANES), jnp.int32),
                   jax.ShapeDtypeStruct((n_tok, D), jnp.bfloat16),
                   jax.ShapeDtypeStruct((N_EXPERTS, n_tok), jnp.float32)],
        compiler_params=pltpu.CompilerParams(vmem_limit_bytes=VMEM_LIMIT),
        name="outproj",
    )(ya, yb, x2, ga, gb, w_bf16, g1, n2, sh2, sc2, w_router_t)


def _route_kernel(lg_ref, eb_ref, eidx_ref, w_ref, rank_ref, cnt_ref, run_ref):
    tn = lg_ref.shape[1]

    @pl.when(pl.program_id(0) == 0)
    def _():
        run_ref[...] = jnp.zeros_like(run_ref)

    scores = 1.0 / (1.0 + jnp.exp(-lg_ref[...]))
    sel = scores + eb_ref[...]
    neg = -jnp.inf
    g_iota = lax.broadcasted_iota(jnp.int32, (GROUP_SIZE, tn), 0)
    gs = []
    for g in range(N_GROUPS):
        blk = sel[g * GROUP_SIZE:(g + 1) * GROUP_SIZE, :]
        m1 = jnp.max(blk, axis=0, keepdims=True)
        i1 = jnp.min(jnp.where(blk == m1, g_iota, GROUP_SIZE), axis=0, keepdims=True)
        m2 = jnp.max(jnp.where(g_iota == i1, neg, blk), axis=0, keepdims=True)
        gs.append(m1 + m2)
    masked = []
    for g in range(N_GROUPS):
        rank = jnp.zeros((1, tn), jnp.float32)
        for gp in range(N_GROUPS):
            if gp == g:
                continue
            beats = (gs[gp] > gs[g]) | ((gs[gp] == gs[g]) & (gp < g))
            rank = rank + jnp.where(beats, 1.0, 0.0)
        keep = rank < TOPK_GROUPS
        masked.append(jnp.where(keep, sel[g * GROUP_SIZE:(g + 1) * GROUP_SIZE, :], neg))
    masked = jnp.concatenate(masked, axis=0)
    e_iota = lax.broadcasted_iota(jnp.int32, (N_EXPERTS, tn), 0)
    idxs, ws, hits = [], [], []
    for _ in range(TOP_K):
        m = jnp.max(masked, axis=0, keepdims=True)
        idx = jnp.min(jnp.where(masked == m, e_iota, N_EXPERTS), axis=0, keepdims=True)
        hit = e_iota == idx
        ws.append(jnp.sum(jnp.where(hit, scores, 0.0), axis=0, keepdims=True))
        masked = jnp.where(hit, neg, masked)
        idxs.append(idx)
        hits.append(hit)
    wsum = ws[0]
    for k in range(1, TOP_K):
        wsum = wsum + ws[k]
    eidx_ref[...] = jnp.concatenate(idxs, axis=0)
    w_ref[...] = jnp.concatenate(ws, axis=0) / wsum * ROUTED_SCALE
    member = jnp.zeros((N_EXPERTS, tn), jnp.float32)
    for hit in hits:
        member = member + jnp.where(hit, 1.0, 0.0)
    before = (lax.broadcasted_iota(jnp.int32, (tn, tn), 0) < lax.broadcasted_iota(jnp.int32, (tn, tn), 1))
    cum = jnp.dot(member.astype(jnp.bfloat16), jnp.where(before, 1.0, 0.0).astype(jnp.bfloat16),
                  preferred_element_type=jnp.float32) + run_ref[...]
    ranks = [jnp.sum(jnp.where(hit, cum, 0.0), axis=0, keepdims=True) for hit in hits]
    rank_ref[...] = jnp.concatenate(ranks, axis=0).astype(jnp.int32)
    total = cum[:, tn - 1:tn] + member[:, tn - 1:tn]
    run_ref[...] = total
    cnt_ref[...] = total.astype(jnp.int32)


def _route_call(lg_t, e_bias):
    n_tok = lg_t.shape[1]
    tn = ROUTE_TN
    tok_blk = pl.BlockSpec((TOP_K, tn), lambda i: (0, i))
    return pl.pallas_call(
        _route_kernel,
        grid=(n_tok // tn,),
        in_specs=[pl.BlockSpec((N_EXPERTS, tn), lambda i: (0, i)),
                  pl.BlockSpec((N_EXPERTS, 1), lambda i: (0, 0))],
        out_specs=[tok_blk, tok_blk, tok_blk, pl.BlockSpec((N_EXPERTS, 1), lambda i: (0, 0))],
        out_shape=[jax.ShapeDtypeStruct((TOP_K, n_tok), jnp.int32),
                   jax.ShapeDtypeStruct((TOP_K, n_tok), jnp.float32),
                   jax.ShapeDtypeStruct((TOP_K, n_tok), jnp.int32),
                   jax.ShapeDtypeStruct((N_EXPERTS, 1), jnp.int32)],
        scratch_shapes=[pltpu.VMEM((N_EXPERTS, 1), jnp.float32)],
        compiler_params=pltpu.CompilerParams(vmem_limit_bytes=VMEM_LIMIT,
                                             dimension_semantics=("arbitrary",)),
        name="route",
    )(lg_t, e_bias.reshape(N_EXPERTS, 1))


def _dest_kernel(eidx_ref, rank_ref, pstart_ref, dest_ref):
    tn = eidx_ref.shape[1]
    e_iota = lax.broadcasted_iota(jnp.int32, (N_EXPERTS, tn), 0)
    pstart = pstart_ref[...]
    rows = []
    for k in range(TOP_K):
        hit = e_iota == eidx_ref[k:k + 1, :]
        rows.append(jnp.sum(jnp.where(hit, pstart, 0), axis=0, keepdims=True))
    dest_ref[...] = jnp.concatenate(rows, axis=0) + rank_ref[...]


def _dest_call(eidx_t, rank_t, pstarts):
    n_tok = eidx_t.shape[1]
    tn = ROUTE_TN
    tok_blk = pl.BlockSpec((TOP_K, tn), lambda i: (0, i))
    return pl.pallas_call(
        _dest_kernel,
        grid=(n_tok // tn,),
        in_specs=[tok_blk, tok_blk, pl.BlockSpec((N_EXPERTS, 1), lambda i: (0, 0))],
        out_specs=tok_blk,
        out_shape=jax.ShapeDtypeStruct((TOP_K, n_tok), jnp.int32),
        name="dest",
    )(eidx_t, rank_t, pstarts.reshape(N_EXPERTS, 1))


def _moe_kernel(blk_e_ref, nvalid_ref, tok_ref, h2p_ref, w1_ref, w3_ref, w2_ref, ys_ref,
                xt, wb1, wb3, wb2):
    i = pl.program_id(0)
    nvalid = nvalid_ref[0]
    stride = MOE_XT_STRIDE

    @pl.when(i < nvalid)
    def _():
        @pl.when((i == 0) | (blk_e_ref[i] != blk_e_ref[jnp.maximum(i - 1, 0)]))
        def _():
            wb1[...] = w1_ref[0].astype(jnp.bfloat16)
            wb3[...] = w3_ref[0].astype(jnp.bfloat16)
            wb2[...] = w2_ref[0].astype(jnp.bfloat16)

        for r in range(MOE_TM):
            src = pl.multiple_of(tok_ref[0, 0, r], PACK_CHUNKS)
            xt[pl.ds(r, PACK_CHUNKS, stride=stride), :] = h2p_ref[pl.ds(src, PACK_CHUNKS), :]
        lo, hi = [], []
        for j in range(PACK_CHUNKS):
            word = xt[pl.ds(j * stride, MOE_TM), :]
            lo.append(lax.bitcast_convert_type(lax.shift_left(word, 16), jnp.float32).astype(jnp.bfloat16))
            hi.append(lax.bitcast_convert_type(word & _HI_MASK, jnp.float32).astype(jnp.bfloat16))
        x = jnp.concatenate(lo + hi, axis=-1)
        a = jnp.dot(x, wb1[...], preferred_element_type=jnp.float32)
        b = jnp.dot(x, wb3[...], preferred_element_type=jnp.float32)
        h = (_silu(a) * b).astype(jnp.bfloat16)
        y = jnp.dot(h, wb2[...], preferred_element_type=jnp.float32)
        for j in range(ROW_CHUNKS):
            ys_ref[:, j, :] = y[:, j * LANES:(j + 1) * LANES]

    @pl.when(i >= nvalid)
    def _():
        ys_ref[...] = jnp.zeros_like(ys_ref)


def _moe_call(blk_e, nvalid, buf_tok3, h2p, w1, w3, w2):
    nblk = buf_tok3.shape[0]
    D = D_MODEL
    grid_spec = pltpu.PrefetchScalarGridSpec(
        num_scalar_prefetch=2,
        grid=(nblk,),
        in_specs=[pl.BlockSpec((1, 1, MOE_TM), lambda i, be, nv: (i, 0, 0), memory_space=pltpu.SMEM),
                  pl.BlockSpec(h2p.shape, lambda i, be, nv: (0, 0), pipeline_mode=pl.Buffered(1)),
                  pl.BlockSpec((1, D, EXPERT_DIM), lambda i, be, nv: (be[i], 0, 0)),
                  pl.BlockSpec((1, D, EXPERT_DIM), lambda i, be, nv: (be[i], 0, 0)),
                  pl.BlockSpec((1, EXPERT_DIM, D), lambda i, be, nv: (be[i], 0, 0))],
        out_specs=pl.BlockSpec((MOE_TM, ROW_CHUNKS, LANES), lambda i, be, nv: (i, 0, 0)),
        scratch_shapes=[pltpu.VMEM((PACK_CHUNKS * MOE_XT_STRIDE, LANES), jnp.int32),
                        pltpu.VMEM((D, EXPERT_DIM), jnp.bfloat16),
                        pltpu.VMEM((D, EXPERT_DIM), jnp.bfloat16),
                        pltpu.VMEM((EXPERT_DIM, D), jnp.bfloat16)],
    )
    return pl.pallas_call(
        _moe_kernel,
        grid_spec=grid_spec,
        out_shape=jax.ShapeDtypeStruct((nblk * MOE_TM, ROW_CHUNKS, LANES), jnp.float32),
        compiler_params=pltpu.CompilerParams(vmem_limit_bytes=MOE_VMEM_LIMIT,
                                             dimension_semantics=("arbitrary",)),
        name="moe",
    )(blk_e, nvalid, buf_tok3, h2p, w1, w3, w2)


def _combine_row_copy(ys_hbm, buf, sem, row, k, t):
    return pltpu.make_async_copy(ys_hbm.at[row], buf.at[k, t], sem.at[0])


def _combine_kernel(dest_ref, ys_hbm, wk_ref, x1_ref, h2_ref, g2_ref, ws1_ref, ws3_ref, ws2_ref, fg_ref,
                    o_ref, buf, sem):
    tm = COMBINE_TM
    for k in range(TOP_K):
        for t in range(tm):
            _combine_row_copy(ys_hbm, buf, sem, dest_ref[0, k, t], k, t).start(priority=t % 2)
    h2 = h2_ref[...]
    a = jnp.dot(h2, ws1_ref[...], preferred_element_type=jnp.float32)
    b = jnp.dot(h2, ws3_ref[...], preferred_element_type=jnp.float32)
    acc = jnp.dot((_silu(a) * b).astype(jnp.bfloat16), ws2_ref[...], preferred_element_type=jnp.float32)
    for k in range(TOP_K):
        for t in range(tm):
            _combine_row_copy(ys_hbm, buf, sem, 0, k, t).wait()
    wk = wk_ref[...]
    for k in range(TOP_K):
        rows = jnp.concatenate([buf[k, :, j, :] for j in range(ROW_CHUNKS)], axis=-1)
        acc = acc + wk[:, k:k + 1] * rows
    x = x1_ref[...] + g2_ref[0] * acc
    o_ref[...] = x * lax.rsqrt(jnp.mean(x * x, axis=-1, keepdims=True) + EPS) * fg_ref[...]


def _combine_call(dest3, ys, wk, x1, h2b, g2, ws1, ws3, ws2, fg, seq):
    n_tok, D = x1.shape
    tm = COMBINE_TM
    per_seq = seq // tm
    row = lambda i: (i, 0)
    const = lambda i: (0, 0)
    return pl.pallas_call(
        _combine_kernel,
        grid=(n_tok // tm,),
        in_specs=[pl.BlockSpec((1, TOP_K, tm), lambda i: (i, 0, 0), memory_space=pltpu.SMEM),
                  pl.BlockSpec(memory_space=pl.ANY),
                  pl.BlockSpec((tm, TOP_K), row),
                  pl.BlockSpec((tm, D), row),
                  pl.BlockSpec((tm, D), row),
                  pl.BlockSpec((1, 1, D), lambda i: (i // per_seq, 0, 0)),
                  pl.BlockSpec((D, SHARED_DIM), const),
                  pl.BlockSpec((D, SHARED_DIM), const),
                  pl.BlockSpec((SHARED_DIM, D), const),
                  pl.BlockSpec((1, D), const)],
        out_specs=pl.BlockSpec((tm, D), row),
        out_shape=jax.ShapeDtypeStruct((n_tok, D), jnp.float32),
        scratch_shapes=[pltpu.VMEM((TOP_K, tm, ROW_CHUNKS, LANES), jnp.float32),
                        pltpu.SemaphoreType.DMA((1,))],
        compiler_params=pltpu.CompilerParams(vmem_limit_bytes=VMEM_LIMIT),
        name="combine",
    )(dest3, ys, wk, x1, h2b, g2, ws1, ws3, ws2, fg)


def _pair_heads(a, axis):
    shape = a.shape
    a = a.reshape(shape[:axis] + (SWA_KV_HEADS, SWA_GROUP, HEAD_DIM) + shape[axis + 1:])
    a = jnp.swapaxes(a, axis, axis + 1)
    return a.reshape(shape)


def _bias_vectors(rel_table, seq):
    vec = rel_table.astype(jnp.float32)[_t5_bucket(jnp.arange(seq, dtype=jnp.int32))]
    m = jnp.arange(2 * SWA_BLOCK)
    d_a = jnp.clip(((-m) % (2 * SWA_BLOCK)) - SWA_BLOCK, 0, seq - 1)
    tab_sw = vec[d_a][:, :SWA_Q_HEADS].T
    nb = seq // MOBA_BLOCK
    d_b = jnp.clip(jnp.arange(nb)[:, None] * MOBA_BLOCK - MOBA_BLOCK + jnp.arange(2 * MOBA_BLOCK)[None, :],
                   0, seq - 1)
    tab_mb = jnp.transpose(vec[d_b][..., SWA_Q_HEADS:], (2, 0, 1))
    return tab_sw, tab_mb


def _route_plan(eidx_t, counts, n_tok):
    A = n_tok * TOP_K
    P = A + N_EXPERTS * MOE_TM
    nblk = P // MOE_TM
    counts = counts.reshape(N_EXPERTS)
    pcounts = (counts + MOE_TM - 1) // MOE_TM * MOE_TM
    pends = jnp.cumsum(pcounts)
    pstarts = pends - pcounts
    blk_e = jnp.minimum(jnp.searchsorted(pends, jnp.arange(nblk, dtype=jnp.int32) * MOE_TM, side='right'),
                        N_EXPERTS - 1).astype(jnp.int32)
    nvalid = (pends[-1] // MOE_TM).astype(jnp.int32).reshape(1)
    slot = (jnp.arange(n_tok, dtype=jnp.int32) * TOP_K)[None, :] + jnp.arange(TOP_K, dtype=jnp.int32)[:, None]
    real_keys = (eidx_t << KEY_SHIFT) | slot
    s = jnp.arange(MOE_TM, dtype=jnp.int32)[None, :]
    e = jnp.arange(N_EXPERTS, dtype=jnp.int32)[:, None]
    pad_keys = jnp.where(s < (pcounts - counts)[:, None], (e << KEY_SHIFT) | (A + s),
                         (N_EXPERTS << KEY_SHIFT) | (A + s))
    keys = jnp.sort(jnp.concatenate([real_keys.reshape(A), pad_keys.reshape(N_EXPERTS * MOE_TM)]))
    slot_sorted = keys & ((1 << KEY_SHIFT) - 1)
    buf_tok = jnp.where(slot_sorted < A, slot_sorted // TOP_K, 0).astype(jnp.int32)
    return (buf_tok * PACK_CHUNKS).reshape(nblk, 1, MOE_TM), blk_e, nvalid, pstarts.astype(jnp.int32)


def kernel(x, c, w_ada, b_ada, norm1_g, w_in, sinks, rel_table, out_norm_a, out_norm_b, w_out, norm2_g,
           w_router, e_bias, w1, w3, w2, ws1, ws3, ws2, final_g):
    B, S, D = x.shape
    assert w_ada.shape[0] == 1, "single-layer stack only"
    assert D == D_MODEL and S % INPROJ_TM == 0 and S % MOBA_BLOCK == 0
    assert B * S * TOP_K + MOE_TM <= (1 << KEY_SHIFT)
    n_tok = B * S
    bf16 = jnp.bfloat16
    drop = lambda a: a.reshape(a.shape[1:])
    tab_sw, tab_mb = _bias_vectors(rel_table, S)
    x2 = x.reshape(n_tok, D)

    mod = _ada_call(c, drop(w_ada), drop(b_ada))
    sh1, sc1, g1, sh2, sc2, g2 = [m.reshape(B, 1, D) for m in jnp.split(mod, 6, axis=-1)]
    w_in2 = drop(w_in)
    w_in_p = jnp.concatenate([_pair_heads(w_in2[:, :SWA_WIDTH], 1), w_in2[:, SWA_WIDTH:]], axis=1).astype(bf16)
    qa, ka, va, qb, kb, vb = _inproj_call(x2, norm1_g.reshape(1, D), sh1, sc1, w_in_p, S)
    sink_col = jnp.repeat(sinks.reshape(SWA_Q_HEADS).astype(jnp.float32), SWA_BLOCK).reshape(
        SWA_KV_HEADS, SWA_GROUP * SWA_BLOCK, 1)
    ya = _swa_call(qa.reshape(B, S, SWA_WIDTH), ka.reshape(B, S, SWA_KV_WIDTH),
                   va.reshape(B, S, SWA_KV_WIDTH), tab_sw, sink_col)
    yb = _moba_call(qb.reshape(B, S, MOBA_WIDTH), kb.reshape(B, S, MOBA_WIDTH),
                    vb.reshape(B, S, MOBA_WIDTH), tab_mb)
    w_out2 = drop(w_out)
    w_out_p = jnp.concatenate([_pair_heads(w_out2[:SWA_WIDTH], 0), w_out2[SWA_WIDTH:]], axis=0).astype(bf16)
    x1, h2p, h2b, lg_t = _outproj_call(
        ya.reshape(n_tok, SWA_WIDTH), yb.reshape(n_tok, MOBA_WIDTH), x2,
        _pair_heads(out_norm_a.reshape(SWA_WIDTH), 0).reshape(1, SWA_WIDTH), out_norm_b.reshape(1, MOBA_WIDTH),
        w_out_p, g1, norm2_g.reshape(1, D), sh2, sc2, drop(w_router).T, S)
    eidx_t, w_t, rank_t, counts = _route_call(lg_t, e_bias.reshape(N_EXPERTS))
    buf_tok3, blk_e, nvalid, pstarts = _route_plan(eidx_t, counts, n_tok)
    dest_t = _dest_call(eidx_t, rank_t, pstarts)
    ys = _moe_call(blk_e, nvalid, buf_tok3, h2p, drop(w1), drop(w3), drop(w2))
    dest3 = jnp.transpose(dest_t.reshape(TOP_K, n_tok // COMBINE_TM, COMBINE_TM), (1, 0, 2))
    out = _combine_call(dest3, ys, w_t.T, x1, h2b, g2, drop(ws1).astype(bf16), drop(ws3).astype(bf16),
                        drop(ws2).astype(bf16), final_g.reshape(1, D), S)
    return out.reshape(B, S, D)
```

```python
import math

import jax
import jax.numpy as jnp
from jax import lax
from jax.experimental import pallas as pl
from jax.experimental.pallas import tpu as pltpu

D_MODEL = 1024
HEAD_DIM = 64
SWA_Q_HEADS = 8
SWA_KV_HEADS = 2
SWA_GROUP = SWA_Q_HEADS // SWA_KV_HEADS
SWA_WINDOW = 128
SWA_BLOCK = 128
SWA_WIDTH = SWA_Q_HEADS * HEAD_DIM
SWA_KV_WIDTH = SWA_KV_HEADS * HEAD_DIM
MOBA_HEADS = 8
MOBA_BLOCK = 256
MOBA_TOPK = 3
MOBA_WIDTH = MOBA_HEADS * HEAD_DIM
N_HEADS = SWA_Q_HEADS + MOBA_HEADS
MIX_WIDTH = SWA_WIDTH + MOBA_WIDTH
IN_COLS = SWA_WIDTH + 2 * SWA_KV_WIDTH + 3 * MOBA_WIDTH
ATTN_SCALE = HEAD_DIM ** -0.5
REL_BUCKETS = 32
REL_MAX_DIST = 1024
N_EXPERTS = 256
TOP_K = 8
N_GROUPS = 8
TOPK_GROUPS = 4
GROUP_SIZE = N_EXPERTS // N_GROUPS
EXPERT_DIM = 256
SHARED_DIM = 256
ROUTED_SCALE = 2.5
EPS = 1e-6
NEG_INF = -1e30

LANES = 128
SUBLANES = 8
ROW_CHUNKS = D_MODEL // LANES
HALF = D_MODEL // 2
PACK_CHUNKS = HALF // LANES

ADA_TN = 512
INPROJ_TM = 512
OUTPROJ_TM = 256
ROUTE_TN = 512
MOE_TM = 256
MOE_XT_STRIDE = MOE_TM + SUBLANES
COMBINE_TM = 128
KEY_SHIFT = 18
VMEM_LIMIT = 48 * 1024 * 1024
MOE_VMEM_LIMIT = 58 * 1024 * 1024

_HIGHEST = lax.Precision.HIGHEST
_NT = (((1,), (1,)), ((), ()))
_TN = (((0,), (0,)), ((), ()))
_HI_MASK = -65536


def _silu(a):
    return a * (1.0 / (1.0 + jnp.exp(-a)))


def _t5_bucket(dist):
    n = jnp.maximum(dist, 0)
    max_exact = REL_BUCKETS // 2
    nf = jnp.maximum(n, 1).astype(jnp.float32)
    large = max_exact + (jnp.log(nf / max_exact) / math.log(REL_MAX_DIST / max_exact)
                         * (REL_BUCKETS - max_exact)).astype(jnp.int32)
    large = jnp.minimum(large, REL_BUCKETS - 1)
    return jnp.where(n < max_exact, n, large)


def _ada_kernel(c_ref, w_ref, b_ref, o_ref):
    c = c_ref[...]
    o_ref[...] = jnp.dot(_silu(c), w_ref[...], precision=_HIGHEST,
                         preferred_element_type=jnp.float32) + b_ref[...]


def _ada_call(c, w_ada, b_ada):
    B, D = c.shape
    n_out = w_ada.shape[1]
    return pl.pallas_call(
        _ada_kernel,
        grid=(n_out // ADA_TN,),
        in_specs=[pl.BlockSpec((B, D), lambda j: (0, 0)),
                  pl.BlockSpec((D, ADA_TN), lambda j: (0, j)),
                  pl.BlockSpec((1, ADA_TN), lambda j: (0, j))],
        out_specs=pl.BlockSpec((B, ADA_TN), lambda j: (0, j)),
        out_shape=jax.ShapeDtypeStruct((B, n_out), jnp.float32),
        name="adaln",
    )(c, w_ada, b_ada.reshape(1, n_out))


def _inproj_kernel(x_ref, g_ref, sh_ref, sc_ref, w_ref, qa_ref, ka_ref, va_ref, qb_ref, kb_ref, vb_ref):
    x = x_ref[...]
    h = x * lax.rsqrt(jnp.mean(x * x, axis=-1, keepdims=True) + EPS) * g_ref[...]
    h = h * (1.0 + sc_ref[0]) + sh_ref[0]
    p = jnp.dot(h.astype(jnp.bfloat16), w_ref[...], preferred_element_type=jnp.float32)
    off = 0
    for ref in (qa_ref, ka_ref, va_ref, qb_ref, kb_ref, vb_ref):
        width = ref.shape[-1]
        ref[...] = p[:, off:off + width].astype(ref.dtype)
        off += width


def _inproj_call(x2, g, sh, sc, w_bf16, seq):
    n_tok, D = x2.shape
    tm = INPROJ_TM
    per_seq = seq // tm
    widths = (SWA_WIDTH, SWA_KV_WIDTH, SWA_KV_WIDTH, MOBA_WIDTH, MOBA_WIDTH, MOBA_WIDTH)
    mod_spec = pl.BlockSpec((1, 1, D), lambda i: (i // per_seq, 0, 0))
    return pl.pallas_call(
        _inproj_kernel,
        grid=(n_tok // tm,),
        in_specs=[pl.BlockSpec((tm, D), lambda i: (i, 0)),
                  pl.BlockSpec((1, D), lambda i: (0, 0)),
                  mod_spec, mod_spec,
                  pl.BlockSpec((D, IN_COLS), lambda i: (0, 0))],
        out_specs=[pl.BlockSpec((tm, w), lambda i: (i, 0)) for w in widths],
        out_shape=[jax.ShapeDtypeStruct((n_tok, w), jnp.bfloat16) for w in widths],
        compiler_params=pltpu.CompilerParams(vmem_limit_bytes=VMEM_LIMIT),
        name="inproj",
    )(x2, g, sh, sc, w_bf16)


def _swa_kernel(q_ref, kc_ref, kp_ref, vc_ref, vp_ref, tab_ref, sink_ref, o_ref, bias_ref):
    n = pl.program_id(1)
    rows = SWA_GROUP * SWA_BLOCK
    lo = lax.broadcasted_iota(jnp.int32, (1, LANES), 1) < HEAD_DIM

    @pl.when((pl.program_id(0) == 0) & (n == 0))
    def _():
        for h in range(SWA_Q_HEADS):
            kvh, c = divmod(h, SWA_GROUP)
            row = jnp.broadcast_to(tab_ref[h:h + 1, :], (SWA_BLOCK, 2 * SWA_BLOCK))
            bias_ref[kvh, c * SWA_BLOCK:(c + 1) * SWA_BLOCK, :] = pltpu.roll(row, 0, 1, stride=1, stride_axis=0)

    kk = jnp.concatenate([kp_ref[0], kc_ref[0]], axis=0)
    vv = jnp.concatenate([vp_ref[0], vc_ref[0]], axis=0)
    qi = lax.broadcasted_iota(jnp.int32, (rows, 2 * SWA_BLOCK), 0) & (SWA_BLOCK - 1)
    kj = lax.broadcasted_iota(jnp.int32, (rows, 2 * SWA_BLOCK), 1)
    dist = qi + SWA_BLOCK - kj
    valid = (dist >= 0) & (dist < SWA_WINDOW) & ((kj >= SWA_BLOCK) | (n > 0))
    qs = jnp.concatenate([q_ref[0, :, c * LANES:(c + 1) * LANES] for c in range(SWA_GROUP)], axis=0)
    masks = (lo, jnp.logical_not(lo))
    scores = [lax.dot_general(jnp.where(hm, qs, jnp.zeros_like(qs)), kk, _NT, preferred_element_type=jnp.float32)
              for hm in masks]
    ps, sink_terms = [], []
    for kvh, s in enumerate(scores):
        s = s * ATTN_SCALE + bias_ref[kvh]
        s = jnp.where(valid, s, NEG_INF)
        sink = sink_ref[kvh]
        m = jnp.maximum(jnp.max(s, axis=-1, keepdims=True), sink)
        ps.append(jnp.exp(s - m).astype(jnp.bfloat16))
        sink_terms.append(jnp.exp(sink - m))
    accs = [jnp.dot(p, jnp.where(hm, vv, jnp.ones_like(vv)), preferred_element_type=jnp.float32)
            for hm, p in zip(masks, ps)]
    outs = [acc / (pltpu.roll(acc, HEAD_DIM, 1) + sink_term) for acc, sink_term in zip(accs, sink_terms)]
    for c in range(SWA_GROUP):
        blk = jnp.where(lo, outs[0][c * SWA_BLOCK:(c + 1) * SWA_BLOCK],
                        outs[1][c * SWA_BLOCK:(c + 1) * SWA_BLOCK])
        o_ref[0, :, c * LANES:(c + 1) * LANES] = blk.astype(o_ref.dtype)


def _swa_call(qa, ka, va, tab_sw, sink_col):
    B, S, _ = qa.shape
    nb = S // SWA_BLOCK
    rows = SWA_GROUP * SWA_BLOCK
    cur = lambda b, n: (b, n, 0)
    prev = lambda b, n: (b, jnp.maximum(n - 1, 0), 0)
    kv_blk = (1, SWA_BLOCK, SWA_KV_WIDTH)
    return pl.pallas_call(
        _swa_kernel,
        grid=(B, nb),
        in_specs=[pl.BlockSpec((1, SWA_BLOCK, SWA_WIDTH), cur),
                  pl.BlockSpec(kv_blk, cur), pl.BlockSpec(kv_blk, prev),
                  pl.BlockSpec(kv_blk, cur), pl.BlockSpec(kv_blk, prev),
                  pl.BlockSpec((SWA_Q_HEADS, 2 * SWA_BLOCK), lambda b, n: (0, 0)),
                  pl.BlockSpec((SWA_KV_HEADS, rows, 1), lambda b, n: (0, 0, 0))],
        out_specs=pl.BlockSpec((1, SWA_BLOCK, SWA_WIDTH), cur),
        out_shape=jax.ShapeDtypeStruct((B, S, SWA_WIDTH), jnp.bfloat16),
        scratch_shapes=[pltpu.VMEM((SWA_KV_HEADS, rows, 2 * SWA_BLOCK), jnp.float32)],
        compiler_params=pltpu.CompilerParams(vmem_limit_bytes=VMEM_LIMIT,
                                             dimension_semantics=("arbitrary", "arbitrary")),
        name="swa",
    )(qa, ka, ka, va, va, tab_sw, sink_col)


def _moba_kernel(q_ref, k_ref, v_ref, tab_ref, o_ref, bias_ref, sel_ref, m_ref, acc_ref):
    S = q_ref.shape[1]
    nb = S // MOBA_BLOCK
    blk = MOBA_BLOCK
    lane_lo = lax.broadcasted_iota(jnp.int32, (1, LANES), 1) < HEAD_DIM
    head_masks = (lane_lo, jnp.logical_not(lane_lo))

    @pl.when(pl.program_id(1) == 0)
    def _():
        for hh in range(2):
            for d in range(nb):
                row = jnp.broadcast_to(tab_ref[hh, d:d + 1, :], (blk, 2 * blk))
                bias_ref[hh, d] = pltpu.roll(row, blk, 1, stride=1, stride_axis=0)[:, :blk]

    kmeans = [jnp.mean(k_ref[0, j * blk:(j + 1) * blk, :].astype(jnp.float32), axis=0, keepdims=True)
              for j in range(nb)]
    kmean = jnp.concatenate(kmeans, axis=0)
    causal_t = (lax.broadcasted_iota(jnp.int32, (blk, blk), 0)
                <= lax.broadcasted_iota(jnp.int32, (blk, blk), 1))
    blk_id = lax.broadcasted_iota(jnp.int32, (nb, blk), 0)

    def rows(ref, b):
        return ref[0, pl.ds(pl.multiple_of(b * blk, blk), blk), :]

    def masked_q(i, hh):
        q = rows(q_ref, i)
        return jnp.where(head_masks[hh], q, jnp.zeros_like(q))

    def masked_v(j, hh):
        v = rows(v_ref, j)
        return jnp.where(head_masks[hh], v, jnp.ones_like(v))

    def own_blocks(q_blocks):
        chains = [(i, hh) for i in q_blocks for hh in range(2)]
        qms = [masked_q(i, hh) for i, hh in chains]
        gates = [lax.dot_general(kmean, qm.astype(jnp.float32), _NT, precision=_HIGHEST,
                                 preferred_element_type=jnp.float32) for qm in qms]
        scores = [lax.dot_general(rows(k_ref, i), qm, _NT, preferred_element_type=jnp.float32)
                  for (i, hh), qm in zip(chains, qms)]
        sels = []
        for (i, hh), gate in zip(chains, gates):
            rank = jnp.zeros((nb, blk), jnp.float32)
            for jp in range(nb):
                gj = gate[jp:jp + 1, :]
                beats = jnp.where(gj > gate, 1.0, jnp.where((gj == gate) & (jp < blk_id), 1.0, 0.0))
                rank = rank + jnp.where(jp < i, beats, 0.0)
            sels.append(jnp.where(rank < MOBA_TOPK, 1.0, 0.0))
        ms, ps = [], []
        for (i, hh), s in zip(chains, scores):
            s = s * ATTN_SCALE + bias_ref[hh, 0]
            s = jnp.where(causal_t, s, NEG_INF)
            m = jnp.max(s, axis=0, keepdims=True)
            ps.append(jnp.exp(s - m).astype(jnp.bfloat16))
            ms.append(m)
        accs = [lax.dot_general(masked_v(i, hh), p, _TN, preferred_element_type=jnp.float32)
                for (i, hh), p in zip(chains, ps)]
        for (i, hh), sel, m, acc in zip(chains, sels, ms, accs):
            sel_ref[2 * i + hh] = sel
            m_ref[2 * i + hh] = jnp.broadcast_to(m, (SUBLANES, blk))
            acc_ref[2 * i + hh] = acc

    def past_blocks(d, q_blocks):
        chains = [(i, hh) for i in q_blocks for hh in range(2)]
        scores = [lax.dot_general(rows(k_ref, i - d), masked_q(i, hh), _NT, preferred_element_type=jnp.float32)
                  for i, hh in chains]
        ms, ps, alphas = [], [], []
        for (i, hh), s in zip(chains, scores):
            s = s * ATTN_SCALE + bias_ref[hh, d]
            picked = jnp.sum(jnp.where(blk_id == i - d, sel_ref[2 * i + hh], 0.0), axis=0, keepdims=True) > 0.5
            s = jnp.where(picked, s, NEG_INF)
            m = m_ref[2 * i + hh][0:1]
            m_new = jnp.maximum(m, jnp.max(s, axis=0, keepdims=True))
            ps.append(jnp.exp(s - m_new).astype(jnp.bfloat16))
            alphas.append(jnp.exp(m - m_new))
            ms.append(m_new)
        pvs = [lax.dot_general(masked_v(i - d, hh), p, _TN, preferred_element_type=jnp.float32)
               for (i, hh), p in zip(chains, ps)]
        accs = [alpha * acc_ref[2 * i + hh] + pv for (i, hh), alpha, pv in zip(chains, alphas, pvs)]
        for (i, hh), m_new, acc in zip(chains, ms, accs):
            m_ref[2 * i + hh] = jnp.broadcast_to(m_new, (SUBLANES, blk))
            acc_ref[2 * i + hh] = acc

    def own_pair(t, carry):
        own_blocks((2 * t, 2 * t + 1))
        return carry

    lax.fori_loop(0, nb // 2, own_pair, 0)

    def diagonal(d, carry):
        def pair(t, c):
            past_blocks(d, (d + 2 * t, d + 2 * t + 1))
            return c

        lax.fori_loop(0, (nb - d) // 2, pair, 0)

        @pl.when((nb - d) % 2 == 1)
        def _():
            past_blocks(d, (nb - 1,))

        return carry

    lax.fori_loop(1, nb, diagonal, 0)

    def finish(i, carry):
        a0 = acc_ref[2 * i]
        a1 = acc_ref[2 * i + 1]
        out_t = jnp.concatenate([a0[:HEAD_DIM] / a0[HEAD_DIM:], a1[HEAD_DIM:] / a1[:HEAD_DIM]], axis=0)
        o_ref[0, pl.ds(pl.multiple_of(i * blk, blk), blk), :] = out_t.T.astype(o_ref.dtype)
        return carry

    lax.fori_loop(0, nb, finish, 0)


def _moba_call(qb, kb, vb, tab_mb):
    B, S, _ = qb.shape
    nb = S // MOBA_BLOCK
    pairs = MOBA_HEADS // 2
    slab = pl.BlockSpec((1, S, LANES), lambda hp, b: (b, 0, hp))
    return pl.pallas_call(
        _moba_kernel,
        grid=(pairs, B),
        in_specs=[slab, slab, slab,
                  pl.BlockSpec((2, nb, 2 * MOBA_BLOCK), lambda hp, b: (hp, 0, 0))],
        out_specs=slab,
        out_shape=jax.ShapeDtypeStruct((B, S, MOBA_WIDTH), jnp.bfloat16),
        scratch_shapes=[pltpu.VMEM((2, nb, MOBA_BLOCK, MOBA_BLOCK), jnp.float32),
                        pltpu.VMEM((2 * nb, nb, MOBA_BLOCK), jnp.float32),
                        pltpu.VMEM((2 * nb, SUBLANES, MOBA_BLOCK), jnp.float32),
                        pltpu.VMEM((2 * nb, LANES, MOBA_BLOCK), jnp.float32)],
        compiler_params=pltpu.CompilerParams(vmem_limit_bytes=VMEM_LIMIT,
                                             dimension_semantics=("arbitrary", "arbitrary")),
        name="moba",
    )(qb, kb, vb, tab_mb)


def _outproj_kernel(ya_ref, yb_ref, x_ref, ga_ref, gb_ref, w_ref, g1_ref, n2_ref, sh_ref, sc_ref, wr_ref,
                    x1_ref, h2p_ref, h2b_ref, lg_ref):
    tm = x_ref.shape[0]

    def norm(y, g):
        y = y.astype(jnp.float32)
        return y * lax.rsqrt(jnp.mean(y * y, axis=-1, keepdims=True) + EPS) * g

    yn = jnp.concatenate([norm(ya_ref[...], ga_ref[...]), norm(yb_ref[...], gb_ref[...])], axis=-1)
    y = jnp.dot(yn.astype(jnp.bfloat16), w_ref[...], preferred_element_type=jnp.float32)
    x1 = x_ref[...] + g1_ref[0] * y
    x1_ref[...] = x1
    h2 = x1 * lax.rsqrt(jnp.mean(x1 * x1, axis=-1, keepdims=True) + EPS) * n2_ref[...]
    h2 = h2 * (1.0 + sc_ref[0]) + sh_ref[0]
    h2b = h2.astype(jnp.bfloat16)
    h2b_ref[...] = h2b
    bits = lax.bitcast_convert_type(h2b.astype(jnp.float32), jnp.int32)
    word = (bits[:, HALF:] & _HI_MASK) | lax.shift_right_logical(bits[:, :HALF], 16)
    for j in range(PACK_CHUNKS):
        h2p_ref[pl.ds(j, tm, stride=PACK_CHUNKS), :] = word[:, j * LANES:(j + 1) * LANES]
    lg_ref[...] = lax.dot_general(wr_ref[...], h2, _NT, precision=_HIGHEST,
                                  preferred_element_type=jnp.float32)


def _outproj_call(ya, yb, x2, ga, gb, w_bf16, g1, n2, sh2, sc2, w_router_t, seq):
    n_tok, D = x2.shape
    tm = OUTPROJ_TM
    per_seq = seq // tm
    row = lambda i: (i, 0)
    const = lambda i: (0, 0)
    mod_spec = pl.BlockSpec((1, 1, D), lambda i: (i // per_seq, 0, 0))
    return pl.pallas_call(
        _outproj_kernel,
        grid=(n_tok // tm,),
        in_specs=[pl.BlockSpec((tm, SWA_WIDTH), row), pl.BlockSpec((tm, MOBA_WIDTH), row),
                  pl.BlockSpec((tm, D), row),
                  pl.BlockSpec((1, SWA_WIDTH), const), pl.BlockSpec((1, MOBA_WIDTH), const),
                  pl.BlockSpec((MIX_WIDTH, D), const),
                  mod_spec, pl.BlockSpec((1, D), const), mod_spec, mod_spec,
                  pl.BlockSpec((N_EXPERTS, D), const)],
        out_specs=[pl.BlockSpec((tm, D), row),
                   pl.BlockSpec((tm * PACK_CHUNKS, LANES), row),
                   pl.BlockSpec((tm, D), row),
                   pl.BlockSpec((N_EXPERTS, tm), lambda i: (0, i))],
        out_shape=[jax.ShapeDtypeStruct((n_tok, D), jnp.float32),
                   jax.ShapeDtypeStruct((n_tok * PACK_CHUNKS, LANES), jnp.int32),
                   jax.ShapeDtypeStruct((n_tok, D), jnp.bfloat16),
                   jax.ShapeDtypeStruct((N_EXPERTS, n_tok), jnp.float32)],
        compiler_params=pltpu.CompilerParams(vmem_limit_bytes=VMEM_LIMIT),
        name="outproj",
    )(ya, yb, x2, ga, gb, w_bf16, g1, n2, sh2, sc2, w_router_t)


def _route_kernel(lg_ref, eb_ref, eidx_ref, w_ref, rank_ref, cnt_ref, run_ref):
    tn = lg_ref.shape[1]

    @pl.when(pl.program_id(0) == 0)
    def _():
        run_ref[...] = jnp.zeros_like(run_ref)

    scores = 1.0 / (1.0 + jnp.exp(-lg_ref[...]))
    sel = scores + eb_ref[...]
    neg = -jnp.inf
    g_iota = lax.broadcasted_iota(jnp.int32, (GROUP_SIZE, tn), 0)
    gs = []
    for g in range(N_GROUPS):
        blk = sel[g * GROUP_SIZE:(g + 1) * GROUP_SIZE, :]
        m1 = jnp.max(blk, axis=0, keepdims=True)
        i1 = jnp.min(jnp.where(blk == m1, g_iota, GROUP_SIZE), axis=0, keepdims=True)
        m2 = jnp.max(jnp.where(g_iota == i1, neg, blk), axis=0, keepdims=True)
        gs.append(m1 + m2)
    masked = []
    for g in range(N_GROUPS):
        rank = jnp.zeros((1, tn), jnp.float32)
        for gp in range(N_GROUPS):
            if gp == g:
                continue
            beats = (gs[gp] > gs[g]) | ((gs[gp] == gs[g]) & (gp < g))
            rank = rank + jnp.where(beats, 1.0, 0.0)
        keep = rank < TOPK_GROUPS
        masked.append(jnp.where(keep, sel[g * GROUP_SIZE:(g + 1) * GROUP_SIZE, :], neg))
    masked = jnp.concatenate(masked, axis=0)
    e_iota = lax.broadcasted_iota(jnp.int32, (N_EXPERTS, tn), 0)
    idxs, ws, hits = [], [], []
    for _ in range(TOP_K):
        m = jnp.max(masked, axis=0, keepdims=True)
        idx = jnp.min(jnp.where(masked == m, e_iota, N_EXPERTS), axis=0, keepdims=True)
        hit = e_iota == idx
        ws.append(jnp.sum(jnp.where(hit, scores, 0.0), axis=0, keepdims=True))
        masked = jnp.where(hit, neg, masked)
        idxs.append(idx)
        hits.append(hit)
    wsum = ws[0]
    for k in range(1, TOP_K):
        wsum = wsum + ws[k]
    eidx_ref[...] = jnp.concatenate(idxs, axis=0)
    w_ref[...] = jnp.concatenate(ws, axis=0) / wsum * ROUTED_SCALE
    member = jnp.zeros((N_EXPERTS, tn), jnp.float32)
    for hit in hits:
        member = member + jnp.where(hit, 1.0, 0.0)
    before = (lax.broadcasted_iota(jnp.int32, (tn, tn), 0) < lax.broadcasted_iota(jnp.int32, (tn, tn), 1))
    cum = jnp.dot(member.astype(jnp.bfloat16), jnp.where(before, 1.0, 0.0).astype(jnp.bfloat16),
                  preferred_element_type=jnp.float32) + run_ref[...]
    ranks = [jnp.sum(jnp.where(hit, cum, 0.0), axis=0, keepdims=True) for hit in hits]
    rank_ref[...] = jnp.concatenate(ranks, axis=0).astype(jnp.int32)
    total = cum[:, tn - 1:tn] + member[:, tn - 1:tn]
    run_ref[...] = total
    cnt_ref[...] = total.astype(jnp.int32)


def _route_call(lg_t, e_bias):
    n_tok = lg_t.shape[1]
    tn = ROUTE_TN
    tok_blk = pl.BlockSpec((TOP_K, tn), lambda i: (0, i))
    return pl.pallas_call(
        _route_kernel,
        grid=(n_tok // tn,),
        in_specs=[pl.BlockSpec((N_EXPERTS, tn), lambda i: (0, i)),
                  pl.BlockSpec((N_EXPERTS, 1), lambda i: (0, 0))],
        out_specs=[tok_blk, tok_blk, tok_blk, pl.BlockSpec((N_EXPERTS, 1), lambda i: (0, 0))],
        out_shape=[jax.ShapeDtypeStruct((TOP_K, n_tok), jnp.int32),
                   jax.ShapeDtypeStruct((TOP_K, n_tok), jnp.float32),
                   jax.ShapeDtypeStruct((TOP_K, n_tok), jnp.int32),
                   jax.ShapeDtypeStruct((N_EXPERTS, 1), jnp.int32)],
        scratch_shapes=[pltpu.VMEM((N_EXPERTS, 1), jnp.float32)],
        compiler_params=pltpu.CompilerParams(vmem_limit_bytes=VMEM_LIMIT,
                                             dimension_semantics=("arbitrary",)),
        name="route",
    )(lg_t, e_bias.reshape(N_EXPERTS, 1))


def _dest_kernel(eidx_ref, rank_ref, pstart_ref, dest_ref):
    tn = eidx_ref.shape[1]
    e_iota = lax.broadcasted_iota(jnp.int32, (N_EXPERTS, tn), 0)
    pstart = pstart_ref[...]
    rows = []
    for k in range(TOP_K):
        hit = e_iota == eidx_ref[k:k + 1, :]
        rows.append(jnp.sum(jnp.where(hit, pstart, 0), axis=0, keepdims=True))
    dest_ref[...] = (jnp.concatenate(rows, axis=0) + rank_ref[...]) * ROW_CHUNKS


def _dest_call(eidx_t, rank_t, pstarts):
    n_tok = eidx_t.shape[1]
    tn = ROUTE_TN
    tok_blk = pl.BlockSpec((TOP_K, tn), lambda i: (0, i))
    return pl.pallas_call(
        _dest_kernel,
        grid=(n_tok // tn,),
        in_specs=[tok_blk, tok_blk, pl.BlockSpec((N_EXPERTS, 1), lambda i: (0, 0))],
        out_specs=tok_blk,
        out_shape=jax.ShapeDtypeStruct((TOP_K, n_tok), jnp.int32),
        name="dest",
    )(eidx_t, rank_t, pstarts.reshape(N_EXPERTS, 1))


def _moe_kernel(blk_e_ref, nvalid_ref, tok_ref, h2p_ref, w1_ref, w3_ref, w2_ref, ys_ref,
                xt, wb1, wb3, wb2):
    i = pl.program_id(0)
    nvalid = nvalid_ref[0]
    stride = MOE_XT_STRIDE

    @pl.when(i < nvalid)
    def _():
        @pl.when((i == 0) | (blk_e_ref[i] != blk_e_ref[jnp.maximum(i - 1, 0)]))
        def _():
            wb1[...] = w1_ref[0].astype(jnp.bfloat16)
            wb3[...] = w3_ref[0].astype(jnp.bfloat16)
            wb2[...] = w2_ref[0].astype(jnp.bfloat16)

        for r in range(MOE_TM):
            src = pl.multiple_of(tok_ref[0, 0, r], PACK_CHUNKS)
            xt[pl.ds(r, PACK_CHUNKS, stride=stride), :] = h2p_ref[pl.ds(src, PACK_CHUNKS), :]
        lo, hi = [], []
        for j in range(PACK_CHUNKS):
            word = xt[pl.ds(j * stride, MOE_TM), :]
            lo.append(lax.bitcast_convert_type(lax.shift_left(word, 16), jnp.float32).astype(jnp.bfloat16))
            hi.append(lax.bitcast_convert_type(word & _HI_MASK, jnp.float32).astype(jnp.bfloat16))
        x = jnp.concatenate(lo + hi, axis=-1)
        a = jnp.dot(x, wb1[...], preferred_element_type=jnp.float32)
        b = jnp.dot(x, wb3[...], preferred_element_type=jnp.float32)
        h = (_silu(a) * b).astype(jnp.bfloat16)
        y = jnp.dot(h, wb2[...], preferred_element_type=jnp.float32)
        for j in range(ROW_CHUNKS):
            ys_ref[pl.ds(j, MOE_TM, stride=ROW_CHUNKS), :] = y[:, j * LANES:(j + 1) * LANES]

    @pl.when(i >= nvalid)
    def _():
        ys_ref[...] = jnp.zeros_like(ys_ref)


def _moe_call(blk_e, nvalid, buf_tok3, h2p, w1, w3, w2):
    nblk = buf_tok3.shape[0]
    D = D_MODEL
    grid_spec = pltpu.PrefetchScalarGridSpec(
        num_scalar_prefetch=2,
        grid=(nblk,),
        in_specs=[pl.BlockSpec((1, 1, MOE_TM), lambda i, be, nv: (i, 0, 0), memory_space=pltpu.SMEM),
                  pl.BlockSpec(h2p.shape, lambda i, be, nv: (0, 0), pipeline_mode=pl.Buffered(1)),
                  pl.BlockSpec((1, D, EXPERT_DIM), lambda i, be, nv: (be[i], 0, 0)),
                  pl.BlockSpec((1, D, EXPERT_DIM), lambda i, be, nv: (be[i], 0, 0)),
                  pl.BlockSpec((1, EXPERT_DIM, D), lambda i, be, nv: (be[i], 0, 0))],
        out_specs=pl.BlockSpec((MOE_TM * ROW_CHUNKS, LANES), lambda i, be, nv: (i, 0)),
        scratch_shapes=[pltpu.VMEM((PACK_CHUNKS * MOE_XT_STRIDE, LANES), jnp.int32),
                        pltpu.VMEM((D, EXPERT_DIM), jnp.bfloat16),
                        pltpu.VMEM((D, EXPERT_DIM), jnp.bfloat16),
                        pltpu.VMEM((EXPERT_DIM, D), jnp.bfloat16)],
    )
    return pl.pallas_call(
        _moe_kernel,
        grid_spec=grid_spec,
        out_shape=jax.ShapeDtypeStruct((nblk * MOE_TM * ROW_CHUNKS, LANES), jnp.float32),
        compiler_params=pltpu.CompilerParams(vmem_limit_bytes=MOE_VMEM_LIMIT,
                                             dimension_semantics=("arbitrary",)),
        name="moe",
    )(blk_e, nvalid, buf_tok3, h2p, w1, w3, w2)


def _combine_row_copy(ys_hbm, buf, sem, src_row, slot, k, t):
    dst_row = (k * COMBINE_TM + t) * ROW_CHUNKS
    return pltpu.make_async_copy(ys_hbm.at[pl.ds(pl.multiple_of(src_row, ROW_CHUNKS), ROW_CHUNKS), :],
                                 buf.at[slot, pl.ds(dst_row, ROW_CHUNKS), :], sem.at[slot])


def _combine_kernel(dest_ref, dest_next_ref, ys_hbm, wk_ref, x1_ref, h2_ref, g2_ref, ws1_ref, ws3_ref,
                    ws2_ref, fg_ref, o_ref, buf, sem):
    tm = COMBINE_TM
    i = pl.program_id(0)
    slot = i % 2

    def issue(d_ref, s):
        for k in range(TOP_K):
            for t in range(tm):
                _combine_row_copy(ys_hbm, buf, sem, d_ref[0, k, t], s, k, t).start(priority=t % 2)

    @pl.when(i == 0)
    def _():
        issue(dest_ref, 0)

    for s in range(2):
        @pl.when((i + 1 < pl.num_programs(0)) & (slot == s))
        def _(s=s):
            issue(dest_next_ref, 1 - s)

    h2 = h2_ref[...]
    a = jnp.dot(h2, ws1_ref[...], preferred_element_type=jnp.float32)
    b = jnp.dot(h2, ws3_ref[...], preferred_element_type=jnp.float32)
    acc = jnp.dot((_silu(a) * b).astype(jnp.bfloat16), ws2_ref[...], preferred_element_type=jnp.float32)
    for k in range(TOP_K):
        for t in range(tm):
            _combine_row_copy(ys_hbm, buf, sem, 0, slot, k, t).wait()
    wk = wk_ref[...]
    for k in range(TOP_K):
        rows = jnp.concatenate(
            [buf[slot, pl.ds(k * tm * ROW_CHUNKS + j, tm, stride=ROW_CHUNKS), :] for j in range(ROW_CHUNKS)],
            axis=-1)
        acc = acc + wk[:, k:k + 1] * rows
    x = x1_ref[...] + g2_ref[0] * acc
    o_ref[...] = x * lax.rsqrt(jnp.mean(x * x, axis=-1, keepdims=True) + EPS) * fg_ref[...]


def _combine_call(dest3, ys, wk, x1, h2b, g2, ws1, ws3, ws2, fg, seq):
    n_tok, D = x1.shape
    tm = COMBINE_TM
    per_seq = seq // tm
    row = lambda i: (i, 0)
    const = lambda i: (0, 0)
    n_steps = n_tok // tm
    return pl.pallas_call(
        _combine_kernel,
        grid=(n_steps,),
        in_specs=[pl.BlockSpec((1, TOP_K, tm), lambda i: (i, 0, 0), memory_space=pltpu.SMEM),
                  pl.BlockSpec((1, TOP_K, tm), lambda i: (jnp.minimum(i + 1, n_steps - 1), 0, 0),
                               memory_space=pltpu.SMEM),
                  pl.BlockSpec(memory_space=pl.ANY),
                  pl.BlockSpec((tm, TOP_K), row),
                  pl.BlockSpec((tm, D), row),
                  pl.BlockSpec((tm, D), row),
                  pl.BlockSpec((1, 1, D), lambda i: (i // per_seq, 0, 0)),
                  pl.BlockSpec((D, SHARED_DIM), const),
                  pl.BlockSpec((D, SHARED_DIM), const),
                  pl.BlockSpec((SHARED_DIM, D), const),
                  pl.BlockSpec((1, D), const)],
        out_specs=pl.BlockSpec((tm, D), row),
        out_shape=jax.ShapeDtypeStruct((n_tok, D), jnp.float32),
        scratch_shapes=[pltpu.VMEM((2, TOP_K * tm * ROW_CHUNKS, LANES), jnp.float32),
                        pltpu.SemaphoreType.DMA((2,))],
        compiler_params=pltpu.CompilerParams(vmem_limit_bytes=VMEM_LIMIT,
                                             dimension_semantics=("arbitrary",)),
        name="combine",
    )(dest3, dest3, ys, wk, x1, h2b, g2, ws1, ws3, ws2, fg)


def _pair_heads(a, axis):
    shape = a.shape
    a = a.reshape(shape[:axis] + (SWA_KV_HEADS, SWA_GROUP, HEAD_DIM) + shape[axis + 1:])
    a = jnp.swapaxes(a, axis, axis + 1)
    return a.reshape(shape)


def _bias_vectors(rel_table, seq):
    vec = rel_table.astype(jnp.float32)[_t5_bucket(jnp.arange(seq, dtype=jnp.int32))]
    m = jnp.arange(2 * SWA_BLOCK)
    d_a = jnp.clip(((-m) % (2 * SWA_BLOCK)) - SWA_BLOCK, 0, seq - 1)
    tab_sw = vec[d_a][:, :SWA_Q_HEADS].T
    nb = seq // MOBA_BLOCK
    d_b = jnp.clip(jnp.arange(nb)[:, None] * MOBA_BLOCK - MOBA_BLOCK + jnp.arange(2 * MOBA_BLOCK)[None, :],
                   0, seq - 1)
    tab_mb = jnp.transpose(vec[d_b][..., SWA_Q_HEADS:], (2, 0, 1))
    return tab_sw, tab_mb


def _route_plan(eidx_t, counts, n_tok):
    A = n_tok * TOP_K
    P = A + N_EXPERTS * MOE_TM
    nblk = P // MOE_TM
    counts = counts.reshape(N_EXPERTS)
    pcounts = (counts + MOE_TM - 1) // MOE_TM * MOE_TM
    pends = jnp.cumsum(pcounts)
    pstarts = pends - pcounts
    blk_start = jnp.arange(nblk, dtype=jnp.int32) * MOE_TM
    blk_e = jnp.minimum(jnp.sum((pends[None, :] <= blk_start[:, None]).astype(jnp.int32), axis=1),
                        N_EXPERTS - 1).astype(jnp.int32)
    nvalid = (pends[-1] // MOE_TM).astype(jnp.int32).reshape(1)
    slot = (jnp.arange(n_tok, dtype=jnp.int32) * TOP_K)[None, :] + jnp.arange(TOP_K, dtype=jnp.int32)[:, None]
    real_keys = (eidx_t << KEY_SHIFT) | slot
    s = jnp.arange(MOE_TM, dtype=jnp.int32)[None, :]
    e = jnp.arange(N_EXPERTS, dtype=jnp.int32)[:, None]
    pad_keys = jnp.where(s < (pcounts - counts)[:, None], (e << KEY_SHIFT) | (A + s),
                         (N_EXPERTS << KEY_SHIFT) | (A + s))
    keys = jnp.sort(jnp.concatenate([real_keys.reshape(A), pad_keys.reshape(N_EXPERTS * MOE_TM)]))
    slot_sorted = keys & ((1 << KEY_SHIFT) - 1)
    buf_tok = jnp.where(slot_sorted < A, slot_sorted // TOP_K, 0).astype(jnp.int32)
    return (buf_tok * PACK_CHUNKS).reshape(nblk, 1, MOE_TM), blk_e, nvalid, pstarts.astype(jnp.int32)


def kernel(x, c, w_ada, b_ada, norm1_g, w_in, sinks, rel_table, out_norm_a, out_norm_b, w_out, norm2_g,
           w_router, e_bias, w1, w3, w2, ws1, ws3, ws2, final_g):
    B, S, D = x.shape
    assert w_ada.shape[0] == 1, "single-layer stack only"
    assert D == D_MODEL and S % INPROJ_TM == 0 and S % MOBA_BLOCK == 0
    assert B * S * TOP_K + MOE_TM <= (1 << KEY_SHIFT)
    n_tok = B * S
    bf16 = jnp.bfloat16
    drop = lambda a: a.reshape(a.shape[1:])
    tab_sw, tab_mb = _bias_vectors(rel_table, S)
    x2 = x.reshape(n_tok, D)

    mod = _ada_call(c, drop(w_ada), drop(b_ada))
    sh1, sc1, g1, sh2, sc2, g2 = [m.reshape(B, 1, D) for m in jnp.split(mod, 6, axis=-1)]
    w_in2 = drop(w_in)
    w_in_p = jnp.concatenate([_pair_heads(w_in2[:, :SWA_WIDTH], 1), w_in2[:, SWA_WIDTH:]], axis=1).astype(bf16)
    qa, ka, va, qb, kb, vb = _inproj_call(x2, norm1_g.reshape(1, D), sh1, sc1, w_in_p, S)
    sink_col = jnp.repeat(sinks.reshape(SWA_Q_HEADS).astype(jnp.float32), SWA_BLOCK).reshape(
        SWA_KV_HEADS, SWA_GROUP * SWA_BLOCK, 1)
    ya = _swa_call(qa.reshape(B, S, SWA_WIDTH), ka.reshape(B, S, SWA_KV_WIDTH),
                   va.reshape(B, S, SWA_KV_WIDTH), tab_sw, sink_col)
    yb = _moba_call(qb.reshape(B, S, MOBA_WIDTH), kb.reshape(B, S, MOBA_WIDTH),
                    vb.reshape(B, S, MOBA_WIDTH), tab_mb)
    w_out2 = drop(w_out)
    w_out_p = jnp.concatenate([_pair_heads(w_out2[:SWA_WIDTH], 0), w_out2[SWA_WIDTH:]], axis=0).astype(bf16)
    x1, h2p, h2b, lg_t = _outproj_call(
        ya.reshape(n_tok, SWA_WIDTH), yb.reshape(n_tok, MOBA_WIDTH), x2,
        _pair_heads(out_norm_a.reshape(SWA_WIDTH), 0).reshape(1, SWA_WIDTH), out_norm_b.reshape(1, MOBA_WIDTH),
        w_out_p, g1, norm2_g.reshape(1, D), sh2, sc2, drop(w_router).T, S)
    eidx_t, w_t, rank_t, counts = _route_call(lg_t, e_bias.reshape(N_EXPERTS))
    buf_tok3, blk_e, nvalid, pstarts = _route_plan(eidx_t, counts, n_tok)
    dest_t = _dest_call(eidx_t, rank_t, pstarts)
    ys = _moe_call(blk_e, nvalid, buf_tok3, h2p, drop(w1), drop(w3), drop(w2))
    dest3 = jnp.transpose(dest_t.reshape(TOP_K, n_tok // COMBINE_TM, COMBINE_TM), (1, 0, 2))
    out = _combine_call(dest3, ys, w_t.T, x1, h2b, g2, drop(ws1).astype(bf16), drop(ws3).astype(bf16),
                        drop(ws2).astype(bf16), final_g.reshape(1, D), S)
    return out.reshape(B, S, D)
```

```python
import math

import jax
import jax.numpy as jnp
from jax import lax
from jax.experimental import pallas as pl
from jax.experimental.pallas import tpu as pltpu

D_MODEL = 1024
HEAD_DIM = 64
SWA_Q_HEADS = 8
SWA_KV_HEADS = 2
SWA_GROUP = SWA_Q_HEADS // SWA_KV_HEADS
SWA_WINDOW = 128
SWA_BLOCK = 128
SWA_WIDTH = SWA_Q_HEADS * HEAD_DIM
SWA_KV_WIDTH = SWA_KV_HEADS * HEAD_DIM
MOBA_HEADS = 8
MOBA_BLOCK = 256
MOBA_TOPK = 3
MOBA_WIDTH = MOBA_HEADS * HEAD_DIM
N_HEADS = SWA_Q_HEADS + MOBA_HEADS
MIX_WIDTH = SWA_WIDTH + MOBA_WIDTH
IN_COLS = SWA_WIDTH + 2 * SWA_KV_WIDTH + 3 * MOBA_WIDTH
ATTN_SCALE = HEAD_DIM ** -0.5
REL_BUCKETS = 32
REL_MAX_DIST = 1024
N_EXPERTS = 256
TOP_K = 8
N_GROUPS = 8
TOPK_GROUPS = 4
GROUP_SIZE = N_EXPERTS // N_GROUPS
EXPERT_DIM = 256
SHARED_DIM = 256
ROUTED_SCALE = 2.5
EPS = 1e-6
NEG_INF = -1e30

LANES = 128
SUBLANES = 8
ROW_CHUNKS = D_MODEL // LANES
HALF = D_MODEL // 2
PACK_CHUNKS = HALF // LANES

ADA_TN = 512
INPROJ_TM = 512
OUTPROJ_TM = 256
ROUTE_TN = 512
MOE_TM = 256
MOE_XT_STRIDE = MOE_TM + SUBLANES
COMBINE_TM = 128
KEY_SHIFT = 18
VMEM_LIMIT = 48 * 1024 * 1024
MOE_VMEM_LIMIT = 58 * 1024 * 1024

_HIGHEST = lax.Precision.HIGHEST
_NT = (((1,), (1,)), ((), ()))
_TN = (((0,), (0,)), ((), ()))
_HI_MASK = -65536


def _silu(a):
    return a * (1.0 / (1.0 + jnp.exp(-a)))


def _t5_bucket(dist):
    n = jnp.maximum(dist, 0)
    max_exact = REL_BUCKETS // 2
    nf = jnp.maximum(n, 1).astype(jnp.float32)
    large = max_exact + (jnp.log(nf / max_exact) / math.log(REL_MAX_DIST / max_exact)
                         * (REL_BUCKETS - max_exact)).astype(jnp.int32)
    large = jnp.minimum(large, REL_BUCKETS - 1)
    return jnp.where(n < max_exact, n, large)


def _ada_kernel(c_ref, w_ref, b_ref, o_ref):
    c = c_ref[...]
    o_ref[...] = jnp.dot(_silu(c), w_ref[...], precision=_HIGHEST,
                         preferred_element_type=jnp.float32) + b_ref[...]


def _ada_call(c, w_ada, b_ada):
    B, D = c.shape
    n_out = w_ada.shape[1]
    return pl.pallas_call(
        _ada_kernel,
        grid=(n_out // ADA_TN,),
        in_specs=[pl.BlockSpec((B, D), lambda j: (0, 0)),
                  pl.BlockSpec((D, ADA_TN), lambda j: (0, j)),
                  pl.BlockSpec((1, ADA_TN), lambda j: (0, j))],
        out_specs=pl.BlockSpec((B, ADA_TN), lambda j: (0, j)),
        out_shape=jax.ShapeDtypeStruct((B, n_out), jnp.float32),
        name="adaln",
    )(c, w_ada, b_ada.reshape(1, n_out))


def _inproj_kernel(x_ref, g_ref, sh_ref, sc_ref, w_ref, qa_ref, ka_ref, va_ref, qb_ref, kb_ref, vb_ref):
    x = x_ref[...]
    h = x * lax.rsqrt(jnp.mean(x * x, axis=-1, keepdims=True) + EPS) * g_ref[...]
    h = h * (1.0 + sc_ref[0]) + sh_ref[0]
    p = jnp.dot(h.astype(jnp.bfloat16), w_ref[...], preferred_element_type=jnp.float32)
    off = 0
    for ref in (qa_ref, ka_ref, va_ref, qb_ref, kb_ref, vb_ref):
        width = ref.shape[-1]
        ref[...] = p[:, off:off + width].astype(ref.dtype)
        off += width


def _inproj_call(x2, g, sh, sc, w_bf16, seq):
    n_tok, D = x2.shape
    tm = INPROJ_TM
    per_seq = seq // tm
    widths = (SWA_WIDTH, SWA_KV_WIDTH, SWA_KV_WIDTH, MOBA_WIDTH, MOBA_WIDTH, MOBA_WIDTH)
    mod_spec = pl.BlockSpec((1, 1, D), lambda i: (i // per_seq, 0, 0))
    return pl.pallas_call(
        _inproj_kernel,
        grid=(n_tok // tm,),
        in_specs=[pl.BlockSpec((tm, D), lambda i: (i, 0)),
                  pl.BlockSpec((1, D), lambda i: (0, 0)),
                  mod_spec, mod_spec,
                  pl.BlockSpec((D, IN_COLS), lambda i: (0, 0))],
        out_specs=[pl.BlockSpec((tm, w), lambda i: (i, 0)) for w in widths],
        out_shape=[jax.ShapeDtypeStruct((n_tok, w), jnp.bfloat16) for w in widths],
        compiler_params=pltpu.CompilerParams(vmem_limit_bytes=VMEM_LIMIT),
        name="inproj",
    )(x2, g, sh, sc, w_bf16)


def _swa_kernel(q_ref, kc_ref, kp_ref, vc_ref, vp_ref, tab_ref, sink_ref, o_ref, bias_ref):
    n = pl.program_id(1)
    rows = SWA_GROUP * SWA_BLOCK
    lo = lax.broadcasted_iota(jnp.int32, (1, LANES), 1) < HEAD_DIM

    @pl.when((pl.program_id(0) == 0) & (n == 0))
    def _():
        for h in range(SWA_Q_HEADS):
            kvh, c = divmod(h, SWA_GROUP)
            row = jnp.broadcast_to(tab_ref[h:h + 1, :], (SWA_BLOCK, 2 * SWA_BLOCK))
            bias_ref[kvh, c * SWA_BLOCK:(c + 1) * SWA_BLOCK, :] = pltpu.roll(row, 0, 1, stride=1, stride_axis=0)

    kk = jnp.concatenate([kp_ref[0], kc_ref[0]], axis=0)
    vv = jnp.concatenate([vp_ref[0], vc_ref[0]], axis=0)
    qi = lax.broadcasted_iota(jnp.int32, (rows, 2 * SWA_BLOCK), 0) & (SWA_BLOCK - 1)
    kj = lax.broadcasted_iota(jnp.int32, (rows, 2 * SWA_BLOCK), 1)
    dist = qi + SWA_BLOCK - kj
    valid = (dist >= 0) & (dist < SWA_WINDOW) & ((kj >= SWA_BLOCK) | (n > 0))
    qs = jnp.concatenate([q_ref[0, :, c * LANES:(c + 1) * LANES] for c in range(SWA_GROUP)], axis=0)
    masks = (lo, jnp.logical_not(lo))
    scores = [lax.dot_general(jnp.where(hm, qs, jnp.zeros_like(qs)), kk, _NT, preferred_element_type=jnp.float32)
              for hm in masks]
    ps, sink_terms = [], []
    for kvh, s in enumerate(scores):
        s = s * ATTN_SCALE + bias_ref[kvh]
        s = jnp.where(valid, s, NEG_INF)
        sink = sink_ref[kvh]
        m = jnp.maximum(jnp.max(s, axis=-1, keepdims=True), sink)
        ps.append(jnp.exp(s - m).astype(jnp.bfloat16))
        sink_terms.append(jnp.exp(sink - m))
    accs = [jnp.dot(p, jnp.where(hm, vv, jnp.ones_like(vv)), preferred_element_type=jnp.float32)
            for hm, p in zip(masks, ps)]
    outs = [acc / (pltpu.roll(acc, HEAD_DIM, 1) + sink_term) for acc, sink_term in zip(accs, sink_terms)]
    for c in range(SWA_GROUP):
        blk = jnp.where(lo, outs[0][c * SWA_BLOCK:(c + 1) * SWA_BLOCK],
                        outs[1][c * SWA_BLOCK:(c + 1) * SWA_BLOCK])
        o_ref[0, :, c * LANES:(c + 1) * LANES] = blk.astype(o_ref.dtype)


def _swa_call(qa, ka, va, tab_sw, sink_col):
    B, S, _ = qa.shape
    nb = S // SWA_BLOCK
    rows = SWA_GROUP * SWA_BLOCK
    cur = lambda b, n: (b, n, 0)
    prev = lambda b, n: (b, jnp.maximum(n - 1, 0), 0)
    kv_blk = (1, SWA_BLOCK, SWA_KV_WIDTH)
    return pl.pallas_call(
        _swa_kernel,
        grid=(B, nb),
        in_specs=[pl.BlockSpec((1, SWA_BLOCK, SWA_WIDTH), cur),
                  pl.BlockSpec(kv_blk, cur), pl.BlockSpec(kv_blk, prev),
                  pl.BlockSpec(kv_blk, cur), pl.BlockSpec(kv_blk, prev),
                  pl.BlockSpec((SWA_Q_HEADS, 2 * SWA_BLOCK), lambda b, n: (0, 0)),
                  pl.BlockSpec((SWA_KV_HEADS, rows, 1), lambda b, n: (0, 0, 0))],
        out_specs=pl.BlockSpec((1, SWA_BLOCK, SWA_WIDTH), cur),
        out_shape=jax.ShapeDtypeStruct((B, S, SWA_WIDTH), jnp.bfloat16),
        scratch_shapes=[pltpu.VMEM((SWA_KV_HEADS, rows, 2 * SWA_BLOCK), jnp.float32)],
        compiler_params=pltpu.CompilerParams(vmem_limit_bytes=VMEM_LIMIT,
                                             dimension_semantics=("arbitrary", "arbitrary")),
        name="swa",
    )(qa, ka, ka, va, va, tab_sw, sink_col)


def _moba_kernel(q_ref, k_ref, v_ref, tab_ref, o_ref, bias_ref, sel_ref, m_ref, acc_ref):
    S = q_ref.shape[1]
    nb = S // MOBA_BLOCK
    blk = MOBA_BLOCK
    lane_lo = lax.broadcasted_iota(jnp.int32, (1, LANES), 1) < HEAD_DIM
    head_masks = (lane_lo, jnp.logical_not(lane_lo))

    @pl.when(pl.program_id(1) == 0)
    def _():
        for hh in range(2):
            for d in range(nb):
                row = jnp.broadcast_to(tab_ref[hh, d:d + 1, :], (blk, 2 * blk))
                bias_ref[hh, d] = pltpu.roll(row, blk, 1, stride=1, stride_axis=0)[:, :blk]

    kmeans = [jnp.mean(k_ref[0, j * blk:(j + 1) * blk, :].astype(jnp.float32), axis=0, keepdims=True)
              for j in range(nb)]
    kmean = jnp.concatenate(kmeans, axis=0)
    causal_t = (lax.broadcasted_iota(jnp.int32, (blk, blk), 0)
                <= lax.broadcasted_iota(jnp.int32, (blk, blk), 1))
    blk_id = lax.broadcasted_iota(jnp.int32, (nb, blk), 0)

    def rows(ref, b):
        return ref[0, pl.ds(pl.multiple_of(b * blk, blk), blk), :]

    def masked_q(i, hh):
        q = rows(q_ref, i)
        return jnp.where(head_masks[hh], q, jnp.zeros_like(q))

    def masked_v(j, hh):
        v = rows(v_ref, j)
        return jnp.where(head_masks[hh], v, jnp.ones_like(v))

    def own_blocks(q_blocks):
        chains = [(i, hh) for i in q_blocks for hh in range(2)]
        qms = [masked_q(i, hh) for i, hh in chains]
        gates = [lax.dot_general(kmean, qm.astype(jnp.float32), _NT, precision=_HIGHEST,
                                 preferred_element_type=jnp.float32) for qm in qms]
        scores = [lax.dot_general(rows(k_ref, i), qm, _NT, preferred_element_type=jnp.float32)
                  for (i, hh), qm in zip(chains, qms)]
        sels = []
        for (i, hh), gate in zip(chains, gates):
            rank = jnp.zeros((nb, blk), jnp.float32)
            for jp in range(nb):
                gj = gate[jp:jp + 1, :]
                beats = jnp.where(gj > gate, 1.0, jnp.where((gj == gate) & (jp < blk_id), 1.0, 0.0))
                rank = rank + jnp.where(jp < i, beats, 0.0)
            sels.append(jnp.where(rank < MOBA_TOPK, 1.0, 0.0))
        ms, ps = [], []
        for (i, hh), s in zip(chains, scores):
            s = s * ATTN_SCALE + bias_ref[hh, 0]
            s = jnp.where(causal_t, s, NEG_INF)
            m = jnp.max(s, axis=0, keepdims=True)
            ps.append(jnp.exp(s - m).astype(jnp.bfloat16))
            ms.append(m)
        accs = [lax.dot_general(masked_v(i, hh), p, _TN, preferred_element_type=jnp.float32)
                for (i, hh), p in zip(chains, ps)]
        for (i, hh), sel, m, acc in zip(chains, sels, ms, accs):
            sel_ref[2 * i + hh] = sel
            m_ref[2 * i + hh] = jnp.broadcast_to(m, (SUBLANES, blk))
            acc_ref[2 * i + hh] = acc

    def past_blocks(d, q_blocks):
        chains = [(i, hh) for i in q_blocks for hh in range(2)]
        scores = [lax.dot_general(rows(k_ref, i - d), masked_q(i, hh), _NT, preferred_element_type=jnp.float32)
                  for i, hh in chains]
        ms, ps, alphas = [], [], []
        for (i, hh), s in zip(chains, scores):
            s = s * ATTN_SCALE + bias_ref[hh, d]
            picked = jnp.sum(jnp.where(blk_id == i - d, sel_ref[2 * i + hh], 0.0), axis=0, keepdims=True) > 0.5
            s = jnp.where(picked, s, NEG_INF)
            m = m_ref[2 * i + hh][0:1]
            m_new = jnp.maximum(m, jnp.max(s, axis=0, keepdims=True))
            ps.append(jnp.exp(s - m_new).astype(jnp.bfloat16))
            alphas.append(jnp.exp(m - m_new))
            ms.append(m_new)
        pvs = [lax.dot_general(masked_v(i - d, hh), p, _TN, preferred_element_type=jnp.float32)
               for (i, hh), p in zip(chains, ps)]
        accs = [alpha * acc_ref[2 * i + hh] + pv for (i, hh), alpha, pv in zip(chains, alphas, pvs)]
        for (i, hh), m_new, acc in zip(chains, ms, accs):
            m_ref[2 * i + hh] = jnp.broadcast_to(m_new, (SUBLANES, blk))
            acc_ref[2 * i + hh] = acc

    def own_pair(t, carry):
        own_blocks((2 * t, 2 * t + 1))
        return carry

    lax.fori_loop(0, nb // 2, own_pair, 0)

    def diagonal(d, carry):
        def pair(t, c):
            past_blocks(d, (d + 2 * t, d + 2 * t + 1))
            return c

        lax.fori_loop(0, (nb - d) // 2, pair, 0)

        @pl.when((nb - d) % 2 == 1)
        def _():
            past_blocks(d, (nb - 1,))

        return carry

    lax.fori_loop(1, nb, diagonal, 0)

    def finish(i, carry):
        a0 = acc_ref[2 * i]
        a1 = acc_ref[2 * i + 1]
        out_t = jnp.concatenate([a0[:HEAD_DIM] / a0[HEAD_DIM:], a1[HEAD_DIM:] / a1[:HEAD_DIM]], axis=0)
        o_ref[0, pl.ds(pl.multiple_of(i * blk, blk), blk), :] = out_t.T.astype(o_ref.dtype)
        return carry

    lax.fori_loop(0, nb, finish, 0)


def _moba_call(qb, kb, vb, tab_mb):
    B, S, _ = qb.shape
    nb = S // MOBA_BLOCK
    pairs = MOBA_HEADS // 2
    slab = pl.BlockSpec((1, S, LANES), lambda hp, b: (b, 0, hp))
    return pl.pallas_call(
        _moba_kernel,
        grid=(pairs, B),
        in_specs=[slab, slab, slab,
                  pl.BlockSpec((2, nb, 2 * MOBA_BLOCK), lambda hp, b: (hp, 0, 0))],
        out_specs=slab,
        out_shape=jax.ShapeDtypeStruct((B, S, MOBA_WIDTH), jnp.bfloat16),
        scratch_shapes=[pltpu.VMEM((2, nb, MOBA_BLOCK, MOBA_BLOCK), jnp.float32),
                        pltpu.VMEM((2 * nb, nb, MOBA_BLOCK), jnp.float32),
                        pltpu.VMEM((2 * nb, SUBLANES, MOBA_BLOCK), jnp.float32),
                        pltpu.VMEM((2 * nb, LANES, MOBA_BLOCK), jnp.float32)],
        compiler_params=pltpu.CompilerParams(vmem_limit_bytes=VMEM_LIMIT,
                                             dimension_semantics=("arbitrary", "arbitrary")),
        name="moba",
    )(qb, kb, vb, tab_mb)


def _outproj_kernel(ya_ref, yb_ref, x_ref, ga_ref, gb_ref, w_ref, g1_ref, n2_ref, sh_ref, sc_ref, wr_hi_ref, wr_lo_ref,
                    x1_ref, h2p_ref, h2b_ref, lg_ref):
    tm = x_ref.shape[0]

    def norm(y, g):
        y = y.astype(jnp.float32)
        return y * lax.rsqrt(jnp.mean(y * y, axis=-1, keepdims=True) + EPS) * g

    yn = jnp.concatenate([norm(ya_ref[...], ga_ref[...]), norm(yb_ref[...], gb_ref[...])], axis=-1)
    y = jnp.dot(yn.astype(jnp.bfloat16), w_ref[...], preferred_element_type=jnp.float32)
    x1 = x_ref[...] + g1_ref[0] * y
    x1_ref[...] = x1
    h2 = x1 * lax.rsqrt(jnp.mean(x1 * x1, axis=-1, keepdims=True) + EPS) * n2_ref[...]
    h2 = h2 * (1.0 + sc_ref[0]) + sh_ref[0]
    h2b = h2.astype(jnp.bfloat16)
    h2b_ref[...] = h2b
    bits = lax.bitcast_convert_type(h2b.astype(jnp.float32), jnp.int32)
    word = (bits[:, HALF:] & _HI_MASK) | lax.shift_right_logical(bits[:, :HALF], 16)
    for j in range(PACK_CHUNKS):
        h2p_ref[pl.ds(j, tm, stride=PACK_CHUNKS), :] = word[:, j * LANES:(j + 1) * LANES]
    h2_lo = (h2 - h2b.astype(jnp.float32)).astype(jnp.bfloat16)
    wr_hi = wr_hi_ref[...]
    lg = lax.dot_general(wr_hi, h2b, _NT, preferred_element_type=jnp.float32)
    lg = lg + lax.dot_general(wr_hi, h2_lo, _NT, preferred_element_type=jnp.float32)
    lg_ref[...] = lg + lax.dot_general(wr_lo_ref[...], h2b, _NT, preferred_element_type=jnp.float32)


def _outproj_call(ya, yb, x2, ga, gb, w_bf16, g1, n2, sh2, sc2, wr_hi, wr_lo, seq):
    n_tok, D = x2.shape
    tm = OUTPROJ_TM
    per_seq = seq // tm
    row = lambda i: (i, 0)
    const = lambda i: (0, 0)
    mod_spec = pl.BlockSpec((1, 1, D), lambda i: (i // per_seq, 0, 0))
    return pl.pallas_call(
        _outproj_kernel,
        grid=(n_tok // tm,),
        in_specs=[pl.BlockSpec((tm, SWA_WIDTH), row), pl.BlockSpec((tm, MOBA_WIDTH), row),
                  pl.BlockSpec((tm, D), row),
                  pl.BlockSpec((1, SWA_WIDTH), const), pl.BlockSpec((1, MOBA_WIDTH), const),
                  pl.BlockSpec((MIX_WIDTH, D), const),
                  mod_spec, pl.BlockSpec((1, D), const), mod_spec, mod_spec,
                  pl.BlockSpec((N_EXPERTS, D), const), pl.BlockSpec((N_EXPERTS, D), const)],
        out_specs=[pl.BlockSpec((tm, D), row),
                   pl.BlockSpec((tm * PACK_CHUNKS, LANES), row),
                   pl.BlockSpec((tm, D), row),
                   pl.BlockSpec((N_EXPERTS, tm), lambda i: (0, i))],
        out_shape=[jax.ShapeDtypeStruct((n_tok, D), jnp.float32),
                   jax.ShapeDtypeStruct((n_tok * PACK_CHUNKS, LANES), jnp.int32),
                   jax.ShapeDtypeStruct((n_tok, D), jnp.bfloat16),
                   jax.ShapeDtypeStruct((N_EXPERTS, n_tok), jnp.float32)],
        compiler_params=pltpu.CompilerParams(vmem_limit_bytes=VMEM_LIMIT),
        name="outproj",
    )(ya, yb, x2, ga, gb, w_bf16, g1, n2, sh2, sc2, wr_hi, wr_lo)


def _route_kernel(lg_ref, eb_ref, before_ref, eidx_ref, w_ref, rank_ref, cnt_ref, run_ref):
    tn = lg_ref.shape[1]

    @pl.when(pl.program_id(0) == 0)
    def _():
        run_ref[...] = jnp.zeros_like(run_ref)

    scores = 1.0 / (1.0 + jnp.exp(-lg_ref[...]))
    sel = scores + eb_ref[...]
    neg = -jnp.inf
    g_iota = lax.broadcasted_iota(jnp.int32, (GROUP_SIZE, tn), 0)
    gs = []
    for g in range(N_GROUPS):
        blk = sel[g * GROUP_SIZE:(g + 1) * GROUP_SIZE, :]
        m1 = jnp.max(blk, axis=0, keepdims=True)
        i1 = jnp.min(jnp.where(blk == m1, g_iota, GROUP_SIZE), axis=0, keepdims=True)
        m2 = jnp.max(jnp.where(g_iota == i1, neg, blk), axis=0, keepdims=True)
        gs.append(m1 + m2)
    masked = []
    for g in range(N_GROUPS):
        rank = jnp.zeros((1, tn), jnp.float32)
        for gp in range(N_GROUPS):
            if gp == g:
                continue
            beats = (gs[gp] > gs[g]) | ((gs[gp] == gs[g]) & (gp < g))
            rank = rank + jnp.where(beats, 1.0, 0.0)
        keep = rank < TOPK_GROUPS
        masked.append(jnp.where(keep, sel[g * GROUP_SIZE:(g + 1) * GROUP_SIZE, :], neg))
    masked = jnp.concatenate(masked, axis=0)
    e_iota = lax.broadcasted_iota(jnp.int32, (N_EXPERTS, tn), 0)
    idxs, ws, hits = [], [], []
    for _ in range(TOP_K):
        m = jnp.max(masked, axis=0, keepdims=True)
        idx = jnp.min(jnp.where(masked == m, e_iota, N_EXPERTS), axis=0, keepdims=True)
        hit = e_iota == idx
        ws.append(jnp.sum(jnp.where(hit, scores, 0.0), axis=0, keepdims=True))
        masked = jnp.where(hit, neg, masked)
        idxs.append(idx)
        hits.append(hit)
    wsum = ws[0]
    for k in range(1, TOP_K):
        wsum = wsum + ws[k]
    eidx_ref[...] = jnp.concatenate(idxs, axis=0)
    w_ref[...] = jnp.concatenate(ws, axis=0) / wsum * ROUTED_SCALE
    member = jnp.zeros((N_EXPERTS, tn), jnp.float32)
    for hit in hits:
        member = member + jnp.where(hit, 1.0, 0.0)
    cum = jnp.dot(member.astype(jnp.bfloat16), before_ref[...],
                  preferred_element_type=jnp.float32) + run_ref[...]
    ranks = [jnp.sum(jnp.where(hit, cum, 0.0), axis=0, keepdims=True) for hit in hits]
    rank_ref[...] = jnp.concatenate(ranks, axis=0).astype(jnp.int32)
    total = cum[:, tn - 1:tn] + member[:, tn - 1:tn]
    run_ref[...] = total
    cnt_ref[...] = total.astype(jnp.int32)


def _route_call(lg_t, e_bias):
    n_tok = lg_t.shape[1]
    tn = ROUTE_TN
    tok_blk = pl.BlockSpec((TOP_K, tn), lambda i: (0, i))
    t_id = jnp.arange(tn, dtype=jnp.int32)
    before = (t_id[:, None] < t_id[None, :]).astype(jnp.bfloat16)
    return pl.pallas_call(
        _route_kernel,
        grid=(n_tok // tn,),
        in_specs=[pl.BlockSpec((N_EXPERTS, tn), lambda i: (0, i)),
                  pl.BlockSpec((N_EXPERTS, 1), lambda i: (0, 0)),
                  pl.BlockSpec((tn, tn), lambda i: (0, 0))],
        out_specs=[tok_blk, tok_blk, tok_blk, pl.BlockSpec((N_EXPERTS, 1), lambda i: (0, 0))],
        out_shape=[jax.ShapeDtypeStruct((TOP_K, n_tok), jnp.int32),
                   jax.ShapeDtypeStruct((TOP_K, n_tok), jnp.float32),
                   jax.ShapeDtypeStruct((TOP_K, n_tok), jnp.int32),
                   jax.ShapeDtypeStruct((N_EXPERTS, 1), jnp.int32)],
        scratch_shapes=[pltpu.VMEM((N_EXPERTS, 1), jnp.float32)],
        compiler_params=pltpu.CompilerParams(vmem_limit_bytes=VMEM_LIMIT,
                                             dimension_semantics=("arbitrary",)),
        name="route",
    )(lg_t, e_bias.reshape(N_EXPERTS, 1), before)


def _dest_kernel(eidx_ref, rank_ref, pstart_ref, dest_ref):
    tn = eidx_ref.shape[1]
    e_iota = lax.broadcasted_iota(jnp.int32, (N_EXPERTS, tn), 0)
    pstart = pstart_ref[...]
    rows = []
    for k in range(TOP_K):
        hit = e_iota == eidx_ref[k:k + 1, :]
        rows.append(jnp.sum(jnp.where(hit, pstart, 0), axis=0, keepdims=True))
    dest_ref[...] = (jnp.concatenate(rows, axis=0) + rank_ref[...]) * ROW_CHUNKS


def _dest_call(eidx_t, rank_t, pstarts):
    n_tok = eidx_t.shape[1]
    tn = ROUTE_TN
    tok_blk = pl.BlockSpec((TOP_K, tn), lambda i: (0, i))
    return pl.pallas_call(
        _dest_kernel,
        grid=(n_tok // tn,),
        in_specs=[tok_blk, tok_blk, pl.BlockSpec((N_EXPERTS, 1), lambda i: (0, 0))],
        out_specs=tok_blk,
        out_shape=jax.ShapeDtypeStruct((TOP_K, n_tok), jnp.int32),
        name="dest",
    )(eidx_t, rank_t, pstarts.reshape(N_EXPERTS, 1))


def _moe_weight_copies(w_hbm, wf, sem, e, slot):
    return [pltpu.make_async_copy(w.at[e], f.at[slot], sem.at[slot]) for w, f in zip(w_hbm, wf)]


def _moe_kernel(blk_e_ref, first_ref, par_ref, next_e_ref, nvalid_ref, tok_ref, h2p_ref,
                w1_hbm, w3_hbm, w2_hbm, ys_ref, xt, wf1, wf3, wf2, wb1, wb3, wb2, sem):
    i = pl.program_id(0)
    nvalid = nvalid_ref[0]
    stride = MOE_XT_STRIDE
    w_hbm = (w1_hbm, w3_hbm, w2_hbm)
    wf = (wf1, wf3, wf2)

    @pl.when(i < nvalid)
    def _():
        @pl.when(i == 0)
        def _():
            for cp in _moe_weight_copies(w_hbm, wf, sem, blk_e_ref[0], 0):
                cp.start()

        @pl.when(first_ref[i] == 1)
        def _():
            slot = par_ref[i]
            for cp in _moe_weight_copies(w_hbm, wf, sem, blk_e_ref[i], slot):
                cp.wait()

            @pl.when(next_e_ref[i] >= 0)
            def _():
                for cp in _moe_weight_copies(w_hbm, wf, sem, next_e_ref[i], 1 - slot):
                    cp.start()

            wb1[...] = wf1[slot].astype(jnp.bfloat16)
            wb3[...] = wf3[slot].astype(jnp.bfloat16)
            wb2[...] = wf2[slot].astype(jnp.bfloat16)

        for r in range(MOE_TM):
            src = pl.multiple_of(tok_ref[0, 0, r], PACK_CHUNKS)
            xt[pl.ds(r, PACK_CHUNKS, stride=stride), :] = h2p_ref[pl.ds(src, PACK_CHUNKS), :]
        lo, hi = [], []
        for j in range(PACK_CHUNKS):
            word = xt[pl.ds(j * stride, MOE_TM), :]
            lo.append(lax.bitcast_convert_type(lax.shift_left(word, 16), jnp.float32).astype(jnp.bfloat16))
            hi.append(lax.bitcast_convert_type(word & _HI_MASK, jnp.float32).astype(jnp.bfloat16))
        x = jnp.concatenate(lo + hi, axis=-1)
        a = jnp.dot(x, wb1[...], preferred_element_type=jnp.float32)
        b = jnp.dot(x, wb3[...], preferred_element_type=jnp.float32)
        h = (_silu(a) * b).astype(jnp.bfloat16)
        y = jnp.dot(h, wb2[...], preferred_element_type=jnp.float32)
        for j in range(ROW_CHUNKS):
            ys_ref[pl.ds(j, MOE_TM, stride=ROW_CHUNKS), :] = y[:, j * LANES:(j + 1) * LANES]

    @pl.when(i >= nvalid)
    def _():
        ys_ref[...] = jnp.zeros_like(ys_ref)


def _moe_call(blk_e, first, parity, next_e, nvalid, buf_tok3, h2p, w1, w3, w2):
    nblk = buf_tok3.shape[0]
    D = D_MODEL
    hbm = pl.BlockSpec(memory_space=pl.ANY)
    grid_spec = pltpu.PrefetchScalarGridSpec(
        num_scalar_prefetch=5,
        grid=(nblk,),
        in_specs=[pl.BlockSpec((1, 1, MOE_TM), lambda i, *_: (i, 0, 0), memory_space=pltpu.SMEM),
                  pl.BlockSpec(h2p.shape, lambda i, *_: (0, 0), pipeline_mode=pl.Buffered(1)),
                  hbm, hbm, hbm],
        out_specs=pl.BlockSpec((MOE_TM * ROW_CHUNKS, LANES), lambda i, *_: (i, 0)),
        scratch_shapes=[pltpu.VMEM((PACK_CHUNKS * MOE_XT_STRIDE, LANES), jnp.int32),
                        pltpu.VMEM((2, D, EXPERT_DIM), jnp.float32),
                        pltpu.VMEM((2, D, EXPERT_DIM), jnp.float32),
                        pltpu.VMEM((2, EXPERT_DIM, D), jnp.float32),
                        pltpu.VMEM((D, EXPERT_DIM), jnp.bfloat16),
                        pltpu.VMEM((D, EXPERT_DIM), jnp.bfloat16),
                        pltpu.VMEM((EXPERT_DIM, D), jnp.bfloat16),
                        pltpu.SemaphoreType.DMA((2,))],
    )
    return pl.pallas_call(
        _moe_kernel,
        grid_spec=grid_spec,
        out_shape=jax.ShapeDtypeStruct((nblk * MOE_TM * ROW_CHUNKS, LANES), jnp.float32),
        compiler_params=pltpu.CompilerParams(vmem_limit_bytes=MOE_VMEM_LIMIT,
                                             dimension_semantics=("arbitrary",)),
        name="moe",
    )(blk_e, first, parity, next_e, nvalid, buf_tok3, h2p, w1, w3, w2)


def _combine_row_copy(ys_hbm, buf, sem, src_row, slot, k, t):
    dst_row = (k * COMBINE_TM + t) * ROW_CHUNKS
    return pltpu.make_async_copy(ys_hbm.at[pl.ds(pl.multiple_of(src_row, ROW_CHUNKS), ROW_CHUNKS), :],
                                 buf.at[slot, pl.ds(dst_row, ROW_CHUNKS), :], sem.at[slot])


def _combine_kernel(dest_ref, dest_next_ref, ys_hbm, wk_ref, x1_ref, h2_ref, g2_ref, ws1_ref, ws3_ref,
                    ws2_ref, fg_ref, o_ref, buf, sem):
    tm = COMBINE_TM
    i = pl.program_id(0)
    slot = i % 2

    def issue(d_ref, s):
        for k in range(TOP_K):
            for t in range(tm):
                _combine_row_copy(ys_hbm, buf, sem, d_ref[0, k, t], s, k, t).start(priority=t % 2)

    @pl.when(i == 0)
    def _():
        issue(dest_ref, 0)

    for s in range(2):
        @pl.when((i + 1 < pl.num_programs(0)) & (slot == s))
        def _(s=s):
            issue(dest_next_ref, 1 - s)

    h2 = h2_ref[...]
    a = jnp.dot(h2, ws1_ref[...], preferred_element_type=jnp.float32)
    b = jnp.dot(h2, ws3_ref[...], preferred_element_type=jnp.float32)
    acc = jnp.dot((_silu(a) * b).astype(jnp.bfloat16), ws2_ref[...], preferred_element_type=jnp.float32)
    for k in range(TOP_K):
        for t in range(tm):
            _combine_row_copy(ys_hbm, buf, sem, 0, slot, k, t).wait()
    wk = wk_ref[...]
    for k in range(TOP_K):
        rows = jnp.concatenate(
            [buf[slot, pl.ds(k * tm * ROW_CHUNKS + j, tm, stride=ROW_CHUNKS), :] for j in range(ROW_CHUNKS)],
            axis=-1)
        acc = acc + wk[:, k:k + 1] * rows
    x = x1_ref[...] + g2_ref[0] * acc
    o_ref[...] = x * lax.rsqrt(jnp.mean(x * x, axis=-1, keepdims=True) + EPS) * fg_ref[...]


def _combine_call(dest3, ys, wk, x1, h2b, g2, ws1, ws3, ws2, fg, seq):
    n_tok, D = x1.shape
    tm = COMBINE_TM
    per_seq = seq // tm
    row = lambda i: (i, 0)
    const = lambda i: (0, 0)
    n_steps = n_tok // tm
    return pl.pallas_call(
        _combine_kernel,
        grid=(n_steps,),
        in_specs=[pl.BlockSpec((1, TOP_K, tm), lambda i: (i, 0, 0), memory_space=pltpu.SMEM),
                  pl.BlockSpec((1, TOP_K, tm), lambda i: (jnp.minimum(i + 1, n_steps - 1), 0, 0),
                               memory_space=pltpu.SMEM),
                  pl.BlockSpec(memory_space=pl.ANY),
                  pl.BlockSpec((tm, TOP_K), row),
                  pl.BlockSpec((tm, D), row),
                  pl.BlockSpec((tm, D), row),
                  pl.BlockSpec((1, 1, D), lambda i: (i // per_seq, 0, 0)),
                  pl.BlockSpec((D, SHARED_DIM), const),
                  pl.BlockSpec((D, SHARED_DIM), const),
                  pl.BlockSpec((SHARED_DIM, D), const),
                  pl.BlockSpec((1, D), const)],
        out_specs=pl.BlockSpec((tm, D), row),
        out_shape=jax.ShapeDtypeStruct((n_tok, D), jnp.float32),
        scratch_shapes=[pltpu.VMEM((2, TOP_K * tm * ROW_CHUNKS, LANES), jnp.float32),
                        pltpu.SemaphoreType.DMA((2,))],
        compiler_params=pltpu.CompilerParams(vmem_limit_bytes=VMEM_LIMIT,
                                             dimension_semantics=("arbitrary",)),
        name="combine",
    )(dest3, dest3, ys, wk, x1, h2b, g2, ws1, ws3, ws2, fg)


def _pair_heads(a, axis):
    shape = a.shape
    a = a.reshape(shape[:axis] + (SWA_KV_HEADS, SWA_GROUP, HEAD_DIM) + shape[axis + 1:])
    a = jnp.swapaxes(a, axis, axis + 1)
    return a.reshape(shape)


def _bias_vectors(rel_table, seq):
    vec = rel_table.astype(jnp.float32)[_t5_bucket(jnp.arange(seq, dtype=jnp.int32))]
    m = jnp.arange(2 * SWA_BLOCK)
    d_a = jnp.clip(((-m) % (2 * SWA_BLOCK)) - SWA_BLOCK, 0, seq - 1)
    tab_sw = vec[d_a][:, :SWA_Q_HEADS].T
    nb = seq // MOBA_BLOCK
    d_b = jnp.clip(jnp.arange(nb)[:, None] * MOBA_BLOCK - MOBA_BLOCK + jnp.arange(2 * MOBA_BLOCK)[None, :],
                   0, seq - 1)
    tab_mb = jnp.transpose(vec[d_b][..., SWA_Q_HEADS:], (2, 0, 1))
    return tab_sw, tab_mb


def _route_plan(eidx_t, counts, n_tok):
    A = n_tok * TOP_K
    P = A + N_EXPERTS * MOE_TM
    nblk = P // MOE_TM
    counts = counts.reshape(N_EXPERTS)
    pcounts = (counts + MOE_TM - 1) // MOE_TM * MOE_TM
    pends = jnp.cumsum(pcounts)
    pstarts = pends - pcounts
    blk_start = jnp.arange(nblk, dtype=jnp.int32) * MOE_TM
    blk_e = jnp.minimum(jnp.sum((pends[None, :] <= blk_start[:, None]).astype(jnp.int32), axis=1),
                        N_EXPERTS - 1).astype(jnp.int32)
    nvalid = (pends[-1] // MOE_TM).astype(jnp.int32).reshape(1)
    first = jnp.concatenate([jnp.ones((1,), jnp.int32), (blk_e[1:] != blk_e[:-1]).astype(jnp.int32)])
    parity = (jnp.cumsum(first) - 1) & 1
    run_end = pends[blk_e] // MOE_TM
    next_e = jnp.where(run_end < nvalid[0], blk_e[jnp.minimum(run_end, nblk - 1)], -1).astype(jnp.int32)
    slot = (jnp.arange(n_tok, dtype=jnp.int32) * TOP_K)[None, :] + jnp.arange(TOP_K, dtype=jnp.int32)[:, None]
    real_keys = (eidx_t << KEY_SHIFT) | slot
    s = jnp.arange(MOE_TM, dtype=jnp.int32)[None, :]
    e = jnp.arange(N_EXPERTS, dtype=jnp.int32)[:, None]
    pad_keys = jnp.where(s < (pcounts - counts)[:, None], (e << KEY_SHIFT) | (A + s),
                         (N_EXPERTS << KEY_SHIFT) | (A + s))
    keys = jnp.sort(jnp.concatenate([real_keys.reshape(A), pad_keys.reshape(N_EXPERTS * MOE_TM)]))
    slot_sorted = keys & ((1 << KEY_SHIFT) - 1)
    buf_tok = jnp.where(slot_sorted < A, slot_sorted // TOP_K, 0).astype(jnp.int32)
    blocks = (blk_e, first, parity.astype(jnp.int32), next_e, nvalid)
    return (buf_tok * PACK_CHUNKS).reshape(nblk, 1, MOE_TM), blocks, pstarts.astype(jnp.int32)


def kernel(x, c, w_ada, b_ada, norm1_g, w_in, sinks, rel_table, out_norm_a, out_norm_b, w_out, norm2_g,
           w_router, e_bias, w1, w3, w2, ws1, ws3, ws2, final_g):
    B, S, D = x.shape
    assert w_ada.shape[0] == 1, "single-layer stack only"
    assert D == D_MODEL and S % INPROJ_TM == 0 and S % MOBA_BLOCK == 0
    assert B * S * TOP_K + MOE_TM <= (1 << KEY_SHIFT)
    n_tok = B * S
    bf16 = jnp.bfloat16
    drop = lambda a: a.reshape(a.shape[1:])
    tab_sw, tab_mb = _bias_vectors(rel_table, S)
    x2 = x.reshape(n_tok, D)

    mod = _ada_call(c, drop(w_ada), drop(b_ada))
    sh1, sc1, g1, sh2, sc2, g2 = [m.reshape(B, 1, D) for m in jnp.split(mod, 6, axis=-1)]
    w_in2 = drop(w_in)
    w_in_p = jnp.concatenate([_pair_heads(w_in2[:, :SWA_WIDTH], 1), w_in2[:, SWA_WIDTH:]], axis=1).astype(bf16)
    qa, ka, va, qb, kb, vb = _inproj_call(x2, norm1_g.reshape(1, D), sh1, sc1, w_in_p, S)
    sink_col = jnp.repeat(sinks.reshape(SWA_Q_HEADS).astype(jnp.float32), SWA_BLOCK).reshape(
        SWA_KV_HEADS, SWA_GROUP * SWA_BLOCK, 1)
    ya = _swa_call(qa.reshape(B, S, SWA_WIDTH), ka.reshape(B, S, SWA_KV_WIDTH),
                   va.reshape(B, S, SWA_KV_WIDTH), tab_sw, sink_col)
    yb = _moba_call(qb.reshape(B, S, MOBA_WIDTH), kb.reshape(B, S, MOBA_WIDTH),
                    vb.reshape(B, S, MOBA_WIDTH), tab_mb)
    w_rt = drop(w_router).T
    wr_hi = w_rt.astype(bf16)
    wr_lo = (w_rt - wr_hi.astype(jnp.float32)).astype(bf16)
    w_out2 = drop(w_out)
    w_out_p = jnp.concatenate([_pair_heads(w_out2[:SWA_WIDTH], 0), w_out2[SWA_WIDTH:]], axis=0).astype(bf16)
    x1, h2p, h2b, lg_t = _outproj_call(
        ya.reshape(n_tok, SWA_WIDTH), yb.reshape(n_tok, MOBA_WIDTH), x2,
        _pair_heads(out_norm_a.reshape(SWA_WIDTH), 0).reshape(1, SWA_WIDTH), out_norm_b.reshape(1, MOBA_WIDTH),
        w_out_p, g1, norm2_g.reshape(1, D), sh2, sc2, wr_hi, wr_lo, S)
    eidx_t, w_t, rank_t, counts = _route_call(lg_t, e_bias.reshape(N_EXPERTS))
    buf_tok3, blocks, pstarts = _route_plan(eidx_t, counts, n_tok)
    dest_t = _dest_call(eidx_t, rank_t, pstarts)
    ys = _moe_call(*blocks, buf_tok3, h2p, drop(w1), drop(w3), drop(w2))
    dest3 = jnp.transpose(dest_t.reshape(TOP_K, n_tok // COMBINE_TM, COMBINE_TM), (1, 0, 2))
    out = _combine_call(dest3, ys, w_t.T, x1, h2b, g2, drop(ws1).astype(bf16), drop(ws3).astype(bf16),
                        drop(ws2).astype(bf16), final_g.reshape(1, D), S)
    return out.reshape(B, S, D)
```

```python
import math

import jax
import jax.numpy as jnp
from jax import lax
from jax.experimental import pallas as pl
from jax.experimental.pallas import tpu as pltpu

D_MODEL = 1024
HEAD_DIM = 64
SWA_Q_HEADS = 8
SWA_KV_HEADS = 2
SWA_GROUP = SWA_Q_HEADS // SWA_KV_HEADS
SWA_WINDOW = 128
SWA_BLOCK = 128
SWA_WIDTH = SWA_Q_HEADS * HEAD_DIM
SWA_KV_WIDTH = SWA_KV_HEADS * HEAD_DIM
MOBA_HEADS = 8
MOBA_BLOCK = 256
MOBA_TOPK = 3
MOBA_WIDTH = MOBA_HEADS * HEAD_DIM
N_HEADS = SWA_Q_HEADS + MOBA_HEADS
MIX_WIDTH = SWA_WIDTH + MOBA_WIDTH
IN_COLS = SWA_WIDTH + 2 * SWA_KV_WIDTH + 3 * MOBA_WIDTH
ATTN_SCALE = HEAD_DIM ** -0.5
REL_BUCKETS = 32
REL_MAX_DIST = 1024
N_EXPERTS = 256
TOP_K = 8
N_GROUPS = 8
TOPK_GROUPS = 4
GROUP_SIZE = N_EXPERTS // N_GROUPS
EXPERT_DIM = 256
SHARED_DIM = 256
ROUTED_SCALE = 2.5
EPS = 1e-6
NEG_INF = -1e30

LANES = 128
SUBLANES = 8
ROW_CHUNKS = D_MODEL // LANES
HALF = D_MODEL // 2
PACK_CHUNKS = HALF // LANES

ADA_TN = 512
INPROJ_TM = 512
OUTPROJ_TM = 256
ROUTE_TN = 512
MOE_TM = 256
MOE_XT_STRIDE = MOE_TM + SUBLANES
COMBINE_TM = 128
KEY_SHIFT = 18
VMEM_LIMIT = 48 * 1024 * 1024
MOE_VMEM_LIMIT = 58 * 1024 * 1024

_HIGHEST = lax.Precision.HIGHEST
_NT = (((1,), (1,)), ((), ()))
_TN = (((0,), (0,)), ((), ()))
_HI_MASK = -65536


def _silu(a):
    return a * (1.0 / (1.0 + jnp.exp(-a)))


def _t5_bucket(dist):
    n = jnp.maximum(dist, 0)
    max_exact = REL_BUCKETS // 2
    nf = jnp.maximum(n, 1).astype(jnp.float32)
    large = max_exact + (jnp.log(nf / max_exact) / math.log(REL_MAX_DIST / max_exact)
                         * (REL_BUCKETS - max_exact)).astype(jnp.int32)
    large = jnp.minimum(large, REL_BUCKETS - 1)
    return jnp.where(n < max_exact, n, large)


def _ada_kernel(c_ref, w_ref, b_ref, o_ref):
    c = c_ref[...]
    o_ref[...] = jnp.dot(_silu(c), w_ref[...], precision=_HIGHEST,
                         preferred_element_type=jnp.float32) + b_ref[...]


def _ada_call(c, w_ada, b_ada):
    B, D = c.shape
    n_out = w_ada.shape[1]
    return pl.pallas_call(
        _ada_kernel,
        grid=(n_out // ADA_TN,),
        in_specs=[pl.BlockSpec((B, D), lambda j: (0, 0)),
                  pl.BlockSpec((D, ADA_TN), lambda j: (0, j)),
                  pl.BlockSpec((1, ADA_TN), lambda j: (0, j))],
        out_specs=pl.BlockSpec((B, ADA_TN), lambda j: (0, j)),
        out_shape=jax.ShapeDtypeStruct((B, n_out), jnp.float32),
        name="adaln",
    )(c, w_ada, b_ada.reshape(1, n_out))


def _inproj_kernel(x_ref, g_ref, sh_ref, sc_ref, w_ref, qa_ref, ka_ref, va_ref, qb_ref, kb_ref, vb_ref):
    x = x_ref[...]
    h = x * lax.rsqrt(jnp.mean(x * x, axis=-1, keepdims=True) + EPS) * g_ref[...]
    h = h * (1.0 + sc_ref[0]) + sh_ref[0]
    p = jnp.dot(h.astype(jnp.bfloat16), w_ref[...], preferred_element_type=jnp.float32)
    off = 0
    for ref in (qa_ref, ka_ref, va_ref, qb_ref, kb_ref, vb_ref):
        width = ref.shape[-1]
        ref[...] = p[:, off:off + width].astype(ref.dtype)
        off += width


def _inproj_call(x2, g, sh, sc, w_bf16, seq):
    n_tok, D = x2.shape
    tm = INPROJ_TM
    per_seq = seq // tm
    widths = (SWA_WIDTH, SWA_KV_WIDTH, SWA_KV_WIDTH, MOBA_WIDTH, MOBA_WIDTH, MOBA_WIDTH)
    mod_spec = pl.BlockSpec((1, 1, D), lambda i: (i // per_seq, 0, 0))
    return pl.pallas_call(
        _inproj_kernel,
        grid=(n_tok // tm,),
        in_specs=[pl.BlockSpec((tm, D), lambda i: (i, 0)),
                  pl.BlockSpec((1, D), lambda i: (0, 0)),
                  mod_spec, mod_spec,
                  pl.BlockSpec((D, IN_COLS), lambda i: (0, 0))],
        out_specs=[pl.BlockSpec((tm, w), lambda i: (i, 0)) for w in widths],
        out_shape=[jax.ShapeDtypeStruct((n_tok, w), jnp.bfloat16) for w in widths],
        compiler_params=pltpu.CompilerParams(vmem_limit_bytes=VMEM_LIMIT),
        name="inproj",
    )(x2, g, sh, sc, w_bf16)


def _swa_kernel(q_ref, kc_ref, kp_ref, vc_ref, vp_ref, tab_ref, sink_ref, o_ref, bias_ref):
    n = pl.program_id(1)
    rows = SWA_GROUP * SWA_BLOCK
    lo = lax.broadcasted_iota(jnp.int32, (1, LANES), 1) < HEAD_DIM

    @pl.when((pl.program_id(0) == 0) & (n == 0))
    def _():
        for h in range(SWA_Q_HEADS):
            kvh, c = divmod(h, SWA_GROUP)
            row = jnp.broadcast_to(tab_ref[h:h + 1, :], (SWA_BLOCK, 2 * SWA_BLOCK))
            bias_ref[kvh, c * SWA_BLOCK:(c + 1) * SWA_BLOCK, :] = pltpu.roll(row, 0, 1, stride=1, stride_axis=0)

    kk = jnp.concatenate([kp_ref[0], kc_ref[0]], axis=0)
    vv = jnp.concatenate([vp_ref[0], vc_ref[0]], axis=0)
    qi = lax.broadcasted_iota(jnp.int32, (rows, 2 * SWA_BLOCK), 0) & (SWA_BLOCK - 1)
    kj = lax.broadcasted_iota(jnp.int32, (rows, 2 * SWA_BLOCK), 1)
    dist = qi + SWA_BLOCK - kj
    valid = (dist >= 0) & (dist < SWA_WINDOW) & ((kj >= SWA_BLOCK) | (n > 0))
    qs = jnp.concatenate([q_ref[0, :, c * LANES:(c + 1) * LANES] for c in range(SWA_GROUP)], axis=0)
    masks = (lo, jnp.logical_not(lo))
    scores = [lax.dot_general(jnp.where(hm, qs, jnp.zeros_like(qs)), kk, _NT, preferred_element_type=jnp.float32)
              for hm in masks]
    ps, sink_terms = [], []
    for kvh, s in enumerate(scores):
        s = s + bias_ref[kvh]
        s = jnp.where(valid, s, NEG_INF)
        sink = sink_ref[kvh]
        m = jnp.maximum(jnp.max(s, axis=-1, keepdims=True), sink)
        ps.append(jnp.exp(s - m).astype(jnp.bfloat16))
        sink_terms.append(jnp.exp(sink - m))
    accs = [jnp.dot(p, jnp.where(hm, vv, jnp.ones_like(vv)), preferred_element_type=jnp.float32)
            for hm, p in zip(masks, ps)]
    outs = [acc / (pltpu.roll(acc, HEAD_DIM, 1) + sink_term) for acc, sink_term in zip(accs, sink_terms)]
    for c in range(SWA_GROUP):
        blk = jnp.where(lo, outs[0][c * SWA_BLOCK:(c + 1) * SWA_BLOCK],
                        outs[1][c * SWA_BLOCK:(c + 1) * SWA_BLOCK])
        o_ref[0, :, c * LANES:(c + 1) * LANES] = blk.astype(o_ref.dtype)


def _swa_call(qa, ka, va, tab_sw, sink_col):
    B, S, _ = qa.shape
    nb = S // SWA_BLOCK
    rows = SWA_GROUP * SWA_BLOCK
    cur = lambda b, n: (b, n, 0)
    prev = lambda b, n: (b, jnp.maximum(n - 1, 0), 0)
    kv_blk = (1, SWA_BLOCK, SWA_KV_WIDTH)
    return pl.pallas_call(
        _swa_kernel,
        grid=(B, nb),
        in_specs=[pl.BlockSpec((1, SWA_BLOCK, SWA_WIDTH), cur),
                  pl.BlockSpec(kv_blk, cur), pl.BlockSpec(kv_blk, prev),
                  pl.BlockSpec(kv_blk, cur), pl.BlockSpec(kv_blk, prev),
                  pl.BlockSpec((SWA_Q_HEADS, 2 * SWA_BLOCK), lambda b, n: (0, 0)),
                  pl.BlockSpec((SWA_KV_HEADS, rows, 1), lambda b, n: (0, 0, 0))],
        out_specs=pl.BlockSpec((1, SWA_BLOCK, SWA_WIDTH), cur),
        out_shape=jax.ShapeDtypeStruct((B, S, SWA_WIDTH), jnp.bfloat16),
        scratch_shapes=[pltpu.VMEM((SWA_KV_HEADS, rows, 2 * SWA_BLOCK), jnp.float32)],
        compiler_params=pltpu.CompilerParams(vmem_limit_bytes=VMEM_LIMIT,
                                             dimension_semantics=("arbitrary", "arbitrary")),
        name="swa",
    )(qa, ka, ka, va, va, tab_sw, sink_col)


def _moba_kernel(q_ref, k_ref, v_ref, tab_ref, o_ref, bias_ref, sel_ref, m_ref, acc_ref):
    S = q_ref.shape[1]
    nb = S // MOBA_BLOCK
    blk = MOBA_BLOCK
    lane_lo = lax.broadcasted_iota(jnp.int32, (1, LANES), 1) < HEAD_DIM
    head_masks = (lane_lo, jnp.logical_not(lane_lo))

    @pl.when(pl.program_id(1) == 0)
    def _():
        for hh in range(2):
            for d in range(nb):
                row = jnp.broadcast_to(tab_ref[hh, d:d + 1, :], (blk, 2 * blk))
                bias_ref[hh, d] = pltpu.roll(row, blk, 1, stride=1, stride_axis=0)[:, :blk]

    kmeans = [jnp.mean(k_ref[0, j * blk:(j + 1) * blk, :].astype(jnp.float32), axis=0, keepdims=True)
              for j in range(nb)]
    kmean = jnp.concatenate(kmeans, axis=0)
    causal_t = (lax.broadcasted_iota(jnp.int32, (blk, blk), 0)
                <= lax.broadcasted_iota(jnp.int32, (blk, blk), 1))
    blk_id = lax.broadcasted_iota(jnp.int32, (nb, blk), 0)

    def rows(ref, b):
        return ref[0, pl.ds(pl.multiple_of(b * blk, blk), blk), :]

    def masked_q(i, hh):
        q = rows(q_ref, i)
        return jnp.where(head_masks[hh], q, jnp.zeros_like(q))

    def masked_v(j, hh):
        v = rows(v_ref, j)
        return jnp.where(head_masks[hh], v, jnp.ones_like(v))

    def own_blocks(q_blocks):
        chains = [(i, hh) for i in q_blocks for hh in range(2)]
        qms = [masked_q(i, hh) for i, hh in chains]
        gates = [lax.dot_general(kmean, qm.astype(jnp.float32), _NT, precision=_HIGHEST,
                                 preferred_element_type=jnp.float32) for qm in qms]
        scores = [lax.dot_general(rows(k_ref, i), qm, _NT, preferred_element_type=jnp.float32)
                  for (i, hh), qm in zip(chains, qms)]
        sels = []
        for (i, hh), gate in zip(chains, gates):
            rank = jnp.zeros((nb, blk), jnp.float32)
            for jp in range(nb):
                gj = gate[jp:jp + 1, :]
                beats = jnp.where(gj > gate, 1.0, jnp.where((gj == gate) & (jp < blk_id), 1.0, 0.0))
                rank = rank + jnp.where(jp < i, beats, 0.0)
            sels.append(jnp.where(rank < MOBA_TOPK, 1.0, 0.0))
        ms, ps = [], []
        for (i, hh), s in zip(chains, scores):
            s = s + bias_ref[hh, 0]
            s = jnp.where(causal_t, s, NEG_INF)
            m = jnp.max(s, axis=0, keepdims=True)
            ps.append(jnp.exp(s - m).astype(jnp.bfloat16))
            ms.append(m)
        accs = [lax.dot_general(masked_v(i, hh), p, _TN, preferred_element_type=jnp.float32)
                for (i, hh), p in zip(chains, ps)]
        for (i, hh), sel, m, acc in zip(chains, sels, ms, accs):
            sel_ref[2 * i + hh] = sel
            m_ref[2 * i + hh] = jnp.broadcast_to(m, (SUBLANES, blk))
            acc_ref[2 * i + hh] = acc

    def past_blocks(d, q_blocks):
        chains = [(i, hh) for i in q_blocks for hh in range(2)]
        scores = [lax.dot_general(rows(k_ref, i - d), masked_q(i, hh), _NT, preferred_element_type=jnp.float32)
                  for i, hh in chains]
        ms, ps, alphas = [], [], []
        for (i, hh), s in zip(chains, scores):
            s = s + bias_ref[hh, d]
            picked = jnp.sum(jnp.where(blk_id == i - d, sel_ref[2 * i + hh], 0.0), axis=0, keepdims=True) > 0.5
            s = jnp.where(picked, s, NEG_INF)
            m = m_ref[2 * i + hh][0:1]
            m_new = jnp.maximum(m, jnp.max(s, axis=0, keepdims=True))
            ps.append(jnp.exp(s - m_new).astype(jnp.bfloat16))
            alphas.append(jnp.exp(m - m_new))
            ms.append(m_new)
        pvs = [lax.dot_general(masked_v(i - d, hh), p, _TN, preferred_element_type=jnp.float32)
               for (i, hh), p in zip(chains, ps)]
        accs = [alpha * acc_ref[2 * i + hh] + pv for (i, hh), alpha, pv in zip(chains, alphas, pvs)]
        for (i, hh), m_new, acc in zip(chains, ms, accs):
            m_ref[2 * i + hh] = jnp.broadcast_to(m_new, (SUBLANES, blk))
            acc_ref[2 * i + hh] = acc

    def own_pair(t, carry):
        own_blocks((2 * t, 2 * t + 1))
        return carry

    lax.fori_loop(0, nb // 2, own_pair, 0)

    def diagonal(d, carry):
        def pair(t, c):
            past_blocks(d, (d + 2 * t, d + 2 * t + 1))
            return c

        lax.fori_loop(0, (nb - d) // 2, pair, 0)

        @pl.when((nb - d) % 2 == 1)
        def _():
            past_blocks(d, (nb - 1,))

        return carry

    lax.fori_loop(1, nb, diagonal, 0)

    def finish(i, carry):
        a0 = acc_ref[2 * i]
        a1 = acc_ref[2 * i + 1]
        out_t = jnp.concatenate([a0[:HEAD_DIM] / a0[HEAD_DIM:], a1[HEAD_DIM:] / a1[:HEAD_DIM]], axis=0)
        o_ref[0, pl.ds(pl.multiple_of(i * blk, blk), blk), :] = out_t.T.astype(o_ref.dtype)
        return carry

    lax.fori_loop(0, nb, finish, 0)


def _moba_call(qb, kb, vb, tab_mb):
    B, S, _ = qb.shape
    nb = S // MOBA_BLOCK
    pairs = MOBA_HEADS // 2
    slab = pl.BlockSpec((1, S, LANES), lambda hp, b: (b, 0, hp))
    return pl.pallas_call(
        _moba_kernel,
        grid=(pairs, B),
        in_specs=[slab, slab, slab,
                  pl.BlockSpec((2, nb, 2 * MOBA_BLOCK), lambda hp, b: (hp, 0, 0))],
        out_specs=slab,
        out_shape=jax.ShapeDtypeStruct((B, S, MOBA_WIDTH), jnp.bfloat16),
        scratch_shapes=[pltpu.VMEM((2, nb, MOBA_BLOCK, MOBA_BLOCK), jnp.float32),
                        pltpu.VMEM((2 * nb, nb, MOBA_BLOCK), jnp.float32),
                        pltpu.VMEM((2 * nb, SUBLANES, MOBA_BLOCK), jnp.float32),
                        pltpu.VMEM((2 * nb, LANES, MOBA_BLOCK), jnp.float32)],
        compiler_params=pltpu.CompilerParams(vmem_limit_bytes=VMEM_LIMIT,
                                             dimension_semantics=("arbitrary", "arbitrary")),
        name="moba",
    )(qb, kb, vb, tab_mb)


def _outproj_kernel(ya_ref, yb_ref, x_ref, ga_ref, gb_ref, w_ref, g1_ref, n2_ref, sh_ref, sc_ref, wr_hi_ref, wr_lo_ref,
                    x1_ref, h2p_ref, h2b_ref, lg_ref):
    tm = x_ref.shape[0]

    def norm(y, g):
        y = y.astype(jnp.float32)
        return y * lax.rsqrt(jnp.mean(y * y, axis=-1, keepdims=True) + EPS) * g

    yn = jnp.concatenate([norm(ya_ref[...], ga_ref[...]), norm(yb_ref[...], gb_ref[...])], axis=-1)
    y = jnp.dot(yn.astype(jnp.bfloat16), w_ref[...], preferred_element_type=jnp.float32)
    x1 = x_ref[...] + g1_ref[0] * y
    x1_ref[...] = x1
    h2 = x1 * lax.rsqrt(jnp.mean(x1 * x1, axis=-1, keepdims=True) + EPS) * n2_ref[...]
    h2 = h2 * (1.0 + sc_ref[0]) + sh_ref[0]
    h2b = h2.astype(jnp.bfloat16)
    h2b_ref[...] = h2b
    bits = lax.bitcast_convert_type(h2b.astype(jnp.float32), jnp.int32)
    word = (bits[:, HALF:] & _HI_MASK) | lax.shift_right_logical(bits[:, :HALF], 16)
    for j in range(PACK_CHUNKS):
        h2p_ref[pl.ds(j, tm, stride=PACK_CHUNKS), :] = word[:, j * LANES:(j + 1) * LANES]
    h2_lo = (h2 - h2b.astype(jnp.float32)).astype(jnp.bfloat16)
    wr_hi = wr_hi_ref[...]
    lg = lax.dot_general(wr_hi, h2b, _NT, preferred_element_type=jnp.float32)
    lg = lg + lax.dot_general(wr_hi, h2_lo, _NT, preferred_element_type=jnp.float32)
    lg_ref[...] = lg + lax.dot_general(wr_lo_ref[...], h2b, _NT, preferred_element_type=jnp.float32)


def _outproj_call(ya, yb, x2, ga, gb, w_bf16, g1, n2, sh2, sc2, wr_hi, wr_lo, seq):
    n_tok, D = x2.shape
    tm = OUTPROJ_TM
    per_seq = seq // tm
    row = lambda i: (i, 0)
    const = lambda i: (0, 0)
    mod_spec = pl.BlockSpec((1, 1, D), lambda i: (i // per_seq, 0, 0))
    return pl.pallas_call(
        _outproj_kernel,
        grid=(n_tok // tm,),
        in_specs=[pl.BlockSpec((tm, SWA_WIDTH), row), pl.BlockSpec((tm, MOBA_WIDTH), row),
                  pl.BlockSpec((tm, D), row),
                  pl.BlockSpec((1, SWA_WIDTH), const), pl.BlockSpec((1, MOBA_WIDTH), const),
                  pl.BlockSpec((MIX_WIDTH, D), const),
                  mod_spec, pl.BlockSpec((1, D), const), mod_spec, mod_spec,
                  pl.BlockSpec((N_EXPERTS, D), const), pl.BlockSpec((N_EXPERTS, D), const)],
        out_specs=[pl.BlockSpec((tm, D), row),
                   pl.BlockSpec((tm * PACK_CHUNKS, LANES), row),
                   pl.BlockSpec((tm, D), row),
                   pl.BlockSpec((N_EXPERTS, tm), lambda i: (0, i))],
        out_shape=[jax.ShapeDtypeStruct((n_tok, D), jnp.float32),
                   jax.ShapeDtypeStruct((n_tok * PACK_CHUNKS, LANES), jnp.int32),
                   jax.ShapeDtypeStruct((n_tok, D), jnp.bfloat16),
                   jax.ShapeDtypeStruct((N_EXPERTS, n_tok), jnp.float32)],
        compiler_params=pltpu.CompilerParams(vmem_limit_bytes=VMEM_LIMIT),
        name="outproj",
    )(ya, yb, x2, ga, gb, w_bf16, g1, n2, sh2, sc2, wr_hi, wr_lo)


def _route_kernel(lg_ref, eb_ref, eidx_ref, w_ref, cnt_ref, run_ref):
    tn = lg_ref.shape[1]

    @pl.when(pl.program_id(0) == 0)
    def _():
        run_ref[...] = jnp.zeros_like(run_ref)

    scores = 1.0 / (1.0 + jnp.exp(-lg_ref[...]))
    sel = scores + eb_ref[...]
    neg = -jnp.inf
    g_iota = lax.broadcasted_iota(jnp.int32, (GROUP_SIZE, tn), 0)
    gs = []
    for g in range(N_GROUPS):
        blk = sel[g * GROUP_SIZE:(g + 1) * GROUP_SIZE, :]
        m1 = jnp.max(blk, axis=0, keepdims=True)
        i1 = jnp.min(jnp.where(blk == m1, g_iota, GROUP_SIZE), axis=0, keepdims=True)
        m2 = jnp.max(jnp.where(g_iota == i1, neg, blk), axis=0, keepdims=True)
        gs.append(m1 + m2)
    masked = []
    for g in range(N_GROUPS):
        rank = jnp.zeros((1, tn), jnp.float32)
        for gp in range(N_GROUPS):
            if gp == g:
                continue
            beats = (gs[gp] > gs[g]) | ((gs[gp] == gs[g]) & (gp < g))
            rank = rank + jnp.where(beats, 1.0, 0.0)
        keep = rank < TOPK_GROUPS
        masked.append(jnp.where(keep, sel[g * GROUP_SIZE:(g + 1) * GROUP_SIZE, :], neg))
    masked = jnp.concatenate(masked, axis=0)
    e_iota = lax.broadcasted_iota(jnp.int32, (N_EXPERTS, tn), 0)
    idxs, ws, hits = [], [], []
    for _ in range(TOP_K):
        m = jnp.max(masked, axis=0, keepdims=True)
        idx = jnp.min(jnp.where(masked == m, e_iota, N_EXPERTS), axis=0, keepdims=True)
        hit = e_iota == idx
        ws.append(jnp.sum(jnp.where(hit, scores, 0.0), axis=0, keepdims=True))
        masked = jnp.where(hit, neg, masked)
        idxs.append(idx)
        hits.append(hit)
    wsum = ws[0]
    for k in range(1, TOP_K):
        wsum = wsum + ws[k]
    eidx_ref[...] = jnp.concatenate(idxs, axis=0)
    w_ref[...] = jnp.concatenate(ws, axis=0) / wsum * ROUTED_SCALE
    member = jnp.zeros((N_EXPERTS, tn), jnp.float32)
    for hit in hits:
        member = member + jnp.where(hit, 1.0, 0.0)
    total = run_ref[...] + jnp.sum(member, axis=1, keepdims=True)
    run_ref[...] = total
    cnt_ref[...] = total.astype(jnp.int32)


def _route_call(lg_t, e_bias):
    n_tok = lg_t.shape[1]
    tn = ROUTE_TN
    tok_blk = pl.BlockSpec((TOP_K, tn), lambda i: (0, i))
    return pl.pallas_call(
        _route_kernel,
        grid=(n_tok // tn,),
        in_specs=[pl.BlockSpec((N_EXPERTS, tn), lambda i: (0, i)),
                  pl.BlockSpec((N_EXPERTS, 1), lambda i: (0, 0))],
        out_specs=[tok_blk, tok_blk, pl.BlockSpec((N_EXPERTS, 1), lambda i: (0, 0))],
        out_shape=[jax.ShapeDtypeStruct((TOP_K, n_tok), jnp.int32),
                   jax.ShapeDtypeStruct((TOP_K, n_tok), jnp.float32),
                   jax.ShapeDtypeStruct((N_EXPERTS, 1), jnp.int32)],
        scratch_shapes=[pltpu.VMEM((N_EXPERTS, 1), jnp.float32)],
        compiler_params=pltpu.CompilerParams(vmem_limit_bytes=VMEM_LIMIT,
                                             dimension_semantics=("arbitrary",)),
        name="route",
    )(lg_t, e_bias.reshape(N_EXPERTS, 1))


def _moe_weight_copies(w_hbm, wf, sem, e, slot):
    return [pltpu.make_async_copy(w.at[e], f.at[slot], sem.at[slot]) for w, f in zip(w_hbm, wf)]


def _moe_row_copy(ybuf, yk_hbm, osem, slot, r, dst_row):
    return pltpu.make_async_copy(ybuf.at[slot, pl.ds(r * ROW_CHUNKS, ROW_CHUNKS), :],
                                 yk_hbm.at[pl.ds(pl.multiple_of(dst_row, ROW_CHUNKS), ROW_CHUNKS), :],
                                 osem.at[slot])


def _moe_kernel(blk_e_ref, first_ref, par_ref, next_e_ref, nvalid_ref, tok_ref, dst_ref, h2p_ref,
                w1_hbm, w3_hbm, w2_hbm, yk_hbm, xt, ybuf, wf1, wf3, wf2, wb1, wb3, wb2, sem, osem):
    i = pl.program_id(0)
    nvalid = nvalid_ref[0]
    stride = MOE_XT_STRIDE
    w_hbm = (w1_hbm, w3_hbm, w2_hbm)
    wf = (wf1, wf3, wf2)
    oslot = i % 2

    def wait_rows(slot):
        for r in range(MOE_TM):
            _moe_row_copy(ybuf, yk_hbm, osem, slot, r, 0).wait()

    @pl.when(i == 0)
    def _():
        ybuf[...] = jnp.zeros_like(ybuf)
        blk_rows = MOE_TM * ROW_CHUNKS
        for s in range(2):
            cp = pltpu.make_async_copy(
                ybuf.at[s], yk_hbm.at[pl.ds(yk_hbm.shape[0] - (2 - s) * blk_rows, blk_rows), :], osem.at[s])
            cp.start()
            cp.wait()

    @pl.when(((i >= 2) & (i < nvalid)) | ((i == nvalid) & (i >= 2)))
    def _():
        wait_rows(oslot)

    @pl.when((i == nvalid) & (i >= 1))
    def _():
        wait_rows(1 - oslot)

    @pl.when(i < nvalid)
    def _():
        @pl.when(i == 0)
        def _():
            for cp in _moe_weight_copies(w_hbm, wf, sem, blk_e_ref[0], 0):
                cp.start()

        @pl.when(first_ref[i] == 1)
        def _():
            slot = par_ref[i]
            for cp in _moe_weight_copies(w_hbm, wf, sem, blk_e_ref[i], slot):
                cp.wait()

            @pl.when(next_e_ref[i] >= 0)
            def _():
                for cp in _moe_weight_copies(w_hbm, wf, sem, next_e_ref[i], 1 - slot):
                    cp.start()

            wb1[...] = wf1[slot].astype(jnp.bfloat16)
            wb3[...] = wf3[slot].astype(jnp.bfloat16)
            wb2[...] = wf2[slot].astype(jnp.bfloat16)

        for r in range(MOE_TM):
            src = pl.multiple_of(tok_ref[0, 0, r], PACK_CHUNKS)
            xt[pl.ds(r, PACK_CHUNKS, stride=stride), :] = h2p_ref[pl.ds(src, PACK_CHUNKS), :]
        lo, hi = [], []
        for j in range(PACK_CHUNKS):
            word = xt[pl.ds(j * stride, MOE_TM), :]
            lo.append(lax.bitcast_convert_type(lax.shift_left(word, 16), jnp.float32).astype(jnp.bfloat16))
            hi.append(lax.bitcast_convert_type(word & _HI_MASK, jnp.float32).astype(jnp.bfloat16))
        x = jnp.concatenate(lo + hi, axis=-1)
        a = jnp.dot(x, wb1[...], preferred_element_type=jnp.float32)
        b = jnp.dot(x, wb3[...], preferred_element_type=jnp.float32)
        h = (_silu(a) * b).astype(jnp.bfloat16)
        y = jnp.dot(h, wb2[...], preferred_element_type=jnp.float32)
        for j in range(ROW_CHUNKS):
            ybuf[oslot, pl.ds(j, MOE_TM, stride=ROW_CHUNKS), :] = y[:, j * LANES:(j + 1) * LANES]
        for r in range(MOE_TM):
            _moe_row_copy(ybuf, yk_hbm, osem, oslot, r, dst_ref[0, 0, r]).start(priority=r % 2)


def _moe_call(blk_e, first, parity, next_e, nvalid, buf_tok3, dst3, h2p, w1, w3, w2, n_rows):
    nblk = buf_tok3.shape[0]
    D = D_MODEL
    hbm = pl.BlockSpec(memory_space=pl.ANY)
    blk_ids = pl.BlockSpec((1, 1, MOE_TM), lambda i, *_: (i, 0, 0), memory_space=pltpu.SMEM)
    grid_spec = pltpu.PrefetchScalarGridSpec(
        num_scalar_prefetch=5,
        grid=(nblk,),
        in_specs=[blk_ids, blk_ids,
                  pl.BlockSpec(h2p.shape, lambda i, *_: (0, 0), pipeline_mode=pl.Buffered(1)),
                  hbm, hbm, hbm],
        out_specs=hbm,
        scratch_shapes=[pltpu.VMEM((PACK_CHUNKS * MOE_XT_STRIDE, LANES), jnp.int32),
                        pltpu.VMEM((2, MOE_TM * ROW_CHUNKS, LANES), jnp.float32),
                        pltpu.VMEM((2, D, EXPERT_DIM), jnp.float32),
                        pltpu.VMEM((2, D, EXPERT_DIM), jnp.float32),
                        pltpu.VMEM((2, EXPERT_DIM, D), jnp.float32),
                        pltpu.VMEM((D, EXPERT_DIM), jnp.bfloat16),
                        pltpu.VMEM((D, EXPERT_DIM), jnp.bfloat16),
                        pltpu.VMEM((EXPERT_DIM, D), jnp.bfloat16),
                        pltpu.SemaphoreType.DMA((2,)),
                        pltpu.SemaphoreType.DMA((2,))],
    )
    return pl.pallas_call(
        _moe_kernel,
        grid_spec=grid_spec,
        out_shape=jax.ShapeDtypeStruct((n_rows * ROW_CHUNKS, LANES), jnp.float32),
        compiler_params=pltpu.CompilerParams(vmem_limit_bytes=MOE_VMEM_LIMIT,
                                             dimension_semantics=("arbitrary",)),
        name="moe",
    )(blk_e, first, parity, next_e, nvalid, buf_tok3, dst3, h2p, w1, w3, w2)


def _combine_kernel(yk_ref, wk_ref, x1_ref, h2_ref, g2_ref, ws1_ref, ws3_ref, ws2_ref, fg_ref, o_ref):
    tm = COMBINE_TM
    h2 = h2_ref[...]
    a = jnp.dot(h2, ws1_ref[...], preferred_element_type=jnp.float32)
    b = jnp.dot(h2, ws3_ref[...], preferred_element_type=jnp.float32)
    acc = jnp.dot((_silu(a) * b).astype(jnp.bfloat16), ws2_ref[...], preferred_element_type=jnp.float32)
    wk = wk_ref[...]
    for k in range(TOP_K):
        rows = jnp.concatenate(
            [yk_ref[pl.ds(k * tm * ROW_CHUNKS + j, tm, stride=ROW_CHUNKS), :] for j in range(ROW_CHUNKS)],
            axis=-1)
        acc = acc + wk[:, k:k + 1] * rows
    x = x1_ref[...] + g2_ref[0] * acc
    o_ref[...] = x * lax.rsqrt(jnp.mean(x * x, axis=-1, keepdims=True) + EPS) * fg_ref[...]


def _combine_call(yk, wk, x1, h2b, g2, ws1, ws3, ws2, fg, seq):
    n_tok, D = x1.shape
    tm = COMBINE_TM
    per_seq = seq // tm
    row = lambda i: (i, 0)
    const = lambda i: (0, 0)
    n_steps = n_tok // tm
    return pl.pallas_call(
        _combine_kernel,
        grid=(n_steps,),
        in_specs=[pl.BlockSpec((TOP_K * tm * ROW_CHUNKS, LANES), row),
                  pl.BlockSpec((tm, TOP_K), row),
                  pl.BlockSpec((tm, D), row),
                  pl.BlockSpec((tm, D), row),
                  pl.BlockSpec((1, 1, D), lambda i: (i // per_seq, 0, 0)),
                  pl.BlockSpec((D, SHARED_DIM), const),
                  pl.BlockSpec((D, SHARED_DIM), const),
                  pl.BlockSpec((SHARED_DIM, D), const),
                  pl.BlockSpec((1, D), const)],
        out_specs=pl.BlockSpec((tm, D), row),
        out_shape=jax.ShapeDtypeStruct((n_tok, D), jnp.float32),
        compiler_params=pltpu.CompilerParams(vmem_limit_bytes=VMEM_LIMIT),
        name="combine",
    )(yk, wk, x1, h2b, g2, ws1, ws3, ws2, fg)


def _pair_heads(a, axis):
    shape = a.shape
    a = a.reshape(shape[:axis] + (SWA_KV_HEADS, SWA_GROUP, HEAD_DIM) + shape[axis + 1:])
    a = jnp.swapaxes(a, axis, axis + 1)
    return a.reshape(shape)


def _bias_vectors(rel_table, seq):
    vec = rel_table.astype(jnp.float32)[_t5_bucket(jnp.arange(seq, dtype=jnp.int32))]
    m = jnp.arange(2 * SWA_BLOCK)
    d_a = jnp.clip(((-m) % (2 * SWA_BLOCK)) - SWA_BLOCK, 0, seq - 1)
    tab_sw = vec[d_a][:, :SWA_Q_HEADS].T
    nb = seq // MOBA_BLOCK
    d_b = jnp.clip(jnp.arange(nb)[:, None] * MOBA_BLOCK - MOBA_BLOCK + jnp.arange(2 * MOBA_BLOCK)[None, :],
                   0, seq - 1)
    tab_mb = jnp.transpose(vec[d_b][..., SWA_Q_HEADS:], (2, 0, 1))
    return tab_sw, tab_mb


def _route_plan(eidx_t, counts, n_tok):
    A = n_tok * TOP_K
    P = A + N_EXPERTS * MOE_TM
    nblk = P // MOE_TM
    counts = counts.reshape(N_EXPERTS)
    pcounts = (counts + MOE_TM - 1) // MOE_TM * MOE_TM
    pends = jnp.cumsum(pcounts)
    pstarts = pends - pcounts
    blk_start = jnp.arange(nblk, dtype=jnp.int32) * MOE_TM
    blk_e = jnp.minimum(jnp.sum((pends[None, :] <= blk_start[:, None]).astype(jnp.int32), axis=1),
                        N_EXPERTS - 1).astype(jnp.int32)
    nvalid = (pends[-1] // MOE_TM).astype(jnp.int32).reshape(1)
    first = jnp.concatenate([jnp.ones((1,), jnp.int32), (blk_e[1:] != blk_e[:-1]).astype(jnp.int32)])
    parity = (jnp.cumsum(first) - 1) & 1
    run_end = pends[blk_e] // MOE_TM
    next_e = jnp.where(run_end < nvalid[0], blk_e[jnp.minimum(run_end, nblk - 1)], -1).astype(jnp.int32)
    slot = (jnp.arange(n_tok, dtype=jnp.int32) * TOP_K)[None, :] + jnp.arange(TOP_K, dtype=jnp.int32)[:, None]
    real_keys = (eidx_t << KEY_SHIFT) | slot
    s = jnp.arange(MOE_TM, dtype=jnp.int32)[None, :]
    e = jnp.arange(N_EXPERTS, dtype=jnp.int32)[:, None]
    pad_keys = jnp.where(s < (pcounts - counts)[:, None], (e << KEY_SHIFT) | (A + s),
                         (N_EXPERTS << KEY_SHIFT) | (A + s))
    keys = jnp.sort(jnp.concatenate([real_keys.reshape(A), pad_keys.reshape(N_EXPERTS * MOE_TM)]))
    slot_sorted = keys & ((1 << KEY_SHIFT) - 1)
    real = slot_sorted < A
    tok = slot_sorted // TOP_K
    buf_tok = jnp.where(real, tok, 0).astype(jnp.int32)
    pick = slot_sorted % TOP_K
    dst = ((tok // COMBINE_TM) * TOP_K + pick) * COMBINE_TM + tok % COMBINE_TM
    row = jnp.arange(P, dtype=jnp.int32)
    spare = A + ((row // MOE_TM) % 2) * MOE_TM + row % MOE_TM
    dst = jnp.where(real, dst, spare).astype(jnp.int32)
    blocks = (blk_e, first, parity.astype(jnp.int32), next_e, nvalid)
    return ((buf_tok * PACK_CHUNKS).reshape(nblk, 1, MOE_TM), (dst * ROW_CHUNKS).reshape(nblk, 1, MOE_TM),
            blocks)


def kernel(x, c, w_ada, b_ada, norm1_g, w_in, sinks, rel_table, out_norm_a, out_norm_b, w_out, norm2_g,
           w_router, e_bias, w1, w3, w2, ws1, ws3, ws2, final_g):
    B, S, D = x.shape
    assert w_ada.shape[0] == 1, "single-layer stack only"
    assert D == D_MODEL and S % INPROJ_TM == 0 and S % MOBA_BLOCK == 0
    assert B * S * TOP_K + MOE_TM <= (1 << KEY_SHIFT)
    n_tok = B * S
    bf16 = jnp.bfloat16
    drop = lambda a: a.reshape(a.shape[1:])
    tab_sw, tab_mb = _bias_vectors(rel_table, S)
    x2 = x.reshape(n_tok, D)

    mod = _ada_call(c, drop(w_ada), drop(b_ada))
    sh1, sc1, g1, sh2, sc2, g2 = [m.reshape(B, 1, D) for m in jnp.split(mod, 6, axis=-1)]
    w_in2 = drop(w_in)
    k_cols = jnp.concatenate([jnp.ones((SWA_WIDTH,)), jnp.full((SWA_KV_WIDTH,), ATTN_SCALE),
                              jnp.ones((SWA_KV_WIDTH + MOBA_WIDTH,)), jnp.full((MOBA_WIDTH,), ATTN_SCALE),
                              jnp.ones((MOBA_WIDTH,))]).astype(jnp.float32)
    w_in_p = (jnp.concatenate([_pair_heads(w_in2[:, :SWA_WIDTH], 1), w_in2[:, SWA_WIDTH:]], axis=1)
              * k_cols[None, :]).astype(bf16)
    qa, ka, va, qb, kb, vb = _inproj_call(x2, norm1_g.reshape(1, D), sh1, sc1, w_in_p, S)
    sink_col = jnp.repeat(sinks.reshape(SWA_Q_HEADS).astype(jnp.float32), SWA_BLOCK).reshape(
        SWA_KV_HEADS, SWA_GROUP * SWA_BLOCK, 1)
    ya = _swa_call(qa.reshape(B, S, SWA_WIDTH), ka.reshape(B, S, SWA_KV_WIDTH),
                   va.reshape(B, S, SWA_KV_WIDTH), tab_sw, sink_col)
    yb = _moba_call(qb.reshape(B, S, MOBA_WIDTH), kb.reshape(B, S, MOBA_WIDTH),
                    vb.reshape(B, S, MOBA_WIDTH), tab_mb)
    w_rt = drop(w_router).T
    wr_hi = w_rt.astype(bf16)
    wr_lo = (w_rt - wr_hi.astype(jnp.float32)).astype(bf16)
    w_out2 = drop(w_out)
    w_out_p = jnp.concatenate([_pair_heads(w_out2[:SWA_WIDTH], 0), w_out2[SWA_WIDTH:]], axis=0).astype(bf16)
    x1, h2p, h2b, lg_t = _outproj_call(
        ya.reshape(n_tok, SWA_WIDTH), yb.reshape(n_tok, MOBA_WIDTH), x2,
        _pair_heads(out_norm_a.reshape(SWA_WIDTH), 0).reshape(1, SWA_WIDTH), out_norm_b.reshape(1, MOBA_WIDTH),
        w_out_p, g1, norm2_g.reshape(1, D), sh2, sc2, wr_hi, wr_lo, S)
    eidx_t, w_t, counts = _route_call(lg_t, e_bias.reshape(N_EXPERTS))
    buf_tok3, dst3, blocks = _route_plan(eidx_t, counts, n_tok)
    yk = _moe_call(*blocks, buf_tok3, dst3, h2p, drop(w1), drop(w3), drop(w2), n_tok * TOP_K + 2 * MOE_TM)
    out = _combine_call(yk, w_t.T, x1, h2b, g2, drop(ws1).astype(bf16), drop(ws3).astype(bf16),
                        drop(ws2).astype(bf16), final_g.reshape(1, D), S)
    return out.reshape(B, S, D)
```

```python
import math

import jax
import jax.numpy as jnp
from jax import lax
from jax.experimental import pallas as pl
from jax.experimental.pallas import tpu as pltpu

D_MODEL = 1024
HEAD_DIM = 64
SWA_Q_HEADS = 8
SWA_KV_HEADS = 2
SWA_GROUP = SWA_Q_HEADS // SWA_KV_HEADS
SWA_WINDOW = 128
SWA_BLOCK = 128
SWA_WIDTH = SWA_Q_HEADS * HEAD_DIM
SWA_KV_WIDTH = SWA_KV_HEADS * HEAD_DIM
MOBA_HEADS = 8
MOBA_BLOCK = 256
MOBA_TOPK = 3
MOBA_WIDTH = MOBA_HEADS * HEAD_DIM
N_HEADS = SWA_Q_HEADS + MOBA_HEADS
MIX_WIDTH = SWA_WIDTH + MOBA_WIDTH
IN_COLS = SWA_WIDTH + 2 * SWA_KV_WIDTH + 3 * MOBA_WIDTH
ATTN_SCALE = HEAD_DIM ** -0.5
REL_BUCKETS = 32
REL_MAX_DIST = 1024
N_EXPERTS = 256
TOP_K = 8
N_GROUPS = 8
TOPK_GROUPS = 4
GROUP_SIZE = N_EXPERTS // N_GROUPS
EXPERT_DIM = 256
SHARED_DIM = 256
ROUTED_SCALE = 2.5
EPS = 1e-6
NEG_INF = -1e30

LANES = 128
SUBLANES = 8
ROW_CHUNKS = D_MODEL // LANES
HALF = D_MODEL // 2
PACK_CHUNKS = HALF // LANES

ADA_TN = 512
INPROJ_TM = 512
OUTPROJ_TM = 256
ROUTE_TN = 512
MOE_TM = 256
MOE_XT_STRIDE = MOE_TM + SUBLANES
MOE_OUT_SLOTS = 3
COMBINE_TM = 128
KEY_SHIFT = 18
VMEM_LIMIT = 48 * 1024 * 1024
MOE_VMEM_LIMIT = 58 * 1024 * 1024

_HIGHEST = lax.Precision.HIGHEST
_NT = (((1,), (1,)), ((), ()))
_TN = (((0,), (0,)), ((), ()))
_HI_MASK = -65536


def _silu(a):
    return a * (1.0 / (1.0 + jnp.exp(-a)))


def _t5_bucket(dist):
    n = jnp.maximum(dist, 0)
    max_exact = REL_BUCKETS // 2
    nf = jnp.maximum(n, 1).astype(jnp.float32)
    large = max_exact + (jnp.log(nf / max_exact) / math.log(REL_MAX_DIST / max_exact)
                         * (REL_BUCKETS - max_exact)).astype(jnp.int32)
    large = jnp.minimum(large, REL_BUCKETS - 1)
    return jnp.where(n < max_exact, n, large)


def _ada_kernel(c_ref, w_ref, b_ref, o_ref):
    c = c_ref[...]
    o_ref[...] = jnp.dot(_silu(c), w_ref[...], precision=_HIGHEST,
                         preferred_element_type=jnp.float32) + b_ref[...]


def _ada_call(c, w_ada, b_ada):
    B, D = c.shape
    n_out = w_ada.shape[1]
    return pl.pallas_call(
        _ada_kernel,
        grid=(n_out // ADA_TN,),
        in_specs=[pl.BlockSpec((B, D), lambda j: (0, 0)),
                  pl.BlockSpec((D, ADA_TN), lambda j: (0, j)),
                  pl.BlockSpec((1, ADA_TN), lambda j: (0, j))],
        out_specs=pl.BlockSpec((B, ADA_TN), lambda j: (0, j)),
        out_shape=jax.ShapeDtypeStruct((B, n_out), jnp.float32),
        name="adaln",
    )(c, w_ada, b_ada.reshape(1, n_out))


def _inproj_kernel(x_ref, g_ref, sh_ref, sc_ref, w_ref, qa_ref, ka_ref, va_ref, qb_ref, kb_ref, vb_ref):
    x = x_ref[...]
    h = x * lax.rsqrt(jnp.mean(x * x, axis=-1, keepdims=True) + EPS) * g_ref[...]
    h = h * (1.0 + sc_ref[0]) + sh_ref[0]
    p = jnp.dot(h.astype(jnp.bfloat16), w_ref[...], preferred_element_type=jnp.float32)
    off = 0
    for ref in (qa_ref, ka_ref, va_ref, qb_ref, kb_ref, vb_ref):
        width = ref.shape[-1]
        ref[...] = p[:, off:off + width].astype(ref.dtype)
        off += width


def _inproj_call(x2, g, sh, sc, w_bf16, seq):
    n_tok, D = x2.shape
    tm = INPROJ_TM
    per_seq = seq // tm
    widths = (SWA_WIDTH, SWA_KV_WIDTH, SWA_KV_WIDTH, MOBA_WIDTH, MOBA_WIDTH, MOBA_WIDTH)
    mod_spec = pl.BlockSpec((1, 1, D), lambda i: (i // per_seq, 0, 0))
    return pl.pallas_call(
        _inproj_kernel,
        grid=(n_tok // tm,),
        in_specs=[pl.BlockSpec((tm, D), lambda i: (i, 0)),
                  pl.BlockSpec((1, D), lambda i: (0, 0)),
                  mod_spec, mod_spec,
                  pl.BlockSpec((D, IN_COLS), lambda i: (0, 0))],
        out_specs=[pl.BlockSpec((tm, w), lambda i: (i, 0)) for w in widths],
        out_shape=[jax.ShapeDtypeStruct((n_tok, w), jnp.bfloat16) for w in widths],
        compiler_params=pltpu.CompilerParams(vmem_limit_bytes=VMEM_LIMIT),
        name="inproj",
    )(x2, g, sh, sc, w_bf16)


def _swa_kernel(q_ref, kc_ref, kp_ref, vc_ref, vp_ref, tab_ref, sink_ref, o_ref, bias_ref):
    n = pl.program_id(1)
    rows = SWA_GROUP * SWA_BLOCK
    lo = lax.broadcasted_iota(jnp.int32, (1, LANES), 1) < HEAD_DIM

    @pl.when((pl.program_id(0) == 0) & (n == 0))
    def _():
        for h in range(SWA_Q_HEADS):
            kvh, c = divmod(h, SWA_GROUP)
            row = jnp.broadcast_to(tab_ref[h:h + 1, :], (SWA_BLOCK, 2 * SWA_BLOCK))
            bias_ref[kvh, c * SWA_BLOCK:(c + 1) * SWA_BLOCK, :] = pltpu.roll(row, 0, 1, stride=1, stride_axis=0)

    kk = jnp.concatenate([kp_ref[0], kc_ref[0]], axis=0)
    vv = jnp.concatenate([vp_ref[0], vc_ref[0]], axis=0)
    qi = lax.broadcasted_iota(jnp.int32, (rows, 2 * SWA_BLOCK), 0) & (SWA_BLOCK - 1)
    kj = lax.broadcasted_iota(jnp.int32, (rows, 2 * SWA_BLOCK), 1)
    dist = qi + SWA_BLOCK - kj
    valid = (dist >= 0) & (dist < SWA_WINDOW) & ((kj >= SWA_BLOCK) | (n > 0))
    qs = jnp.concatenate([q_ref[0, :, c * LANES:(c + 1) * LANES] for c in range(SWA_GROUP)], axis=0)
    masks = (lo, jnp.logical_not(lo))
    scores = [lax.dot_general(jnp.where(hm, qs, jnp.zeros_like(qs)), kk, _NT, preferred_element_type=jnp.float32)
              for hm in masks]
    ps, sink_terms = [], []
    for kvh, s in enumerate(scores):
        s = s + bias_ref[kvh]
        s = jnp.where(valid, s, NEG_INF)
        sink = sink_ref[kvh]
        m = jnp.maximum(jnp.max(s, axis=-1, keepdims=True), sink)
        ps.append(jnp.exp(s - m).astype(jnp.bfloat16))
        sink_terms.append(jnp.exp(sink - m))
    accs = [jnp.dot(p, jnp.where(hm, vv, jnp.ones_like(vv)), preferred_element_type=jnp.float32)
            for hm, p in zip(masks, ps)]
    outs = [acc / (pltpu.roll(acc, HEAD_DIM, 1) + sink_term) for acc, sink_term in zip(accs, sink_terms)]
    for c in range(SWA_GROUP):
        blk = jnp.where(lo, outs[0][c * SWA_BLOCK:(c + 1) * SWA_BLOCK],
                        outs[1][c * SWA_BLOCK:(c + 1) * SWA_BLOCK])
        o_ref[0, :, c * LANES:(c + 1) * LANES] = blk.astype(o_ref.dtype)


def _swa_call(qa, ka, va, tab_sw, sink_col):
    B, S, _ = qa.shape
    nb = S // SWA_BLOCK
    rows = SWA_GROUP * SWA_BLOCK
    cur = lambda b, n: (b, n, 0)
    prev = lambda b, n: (b, jnp.maximum(n - 1, 0), 0)
    kv_blk = (1, SWA_BLOCK, SWA_KV_WIDTH)
    return pl.pallas_call(
        _swa_kernel,
        grid=(B, nb),
        in_specs=[pl.BlockSpec((1, SWA_BLOCK, SWA_WIDTH), cur),
                  pl.BlockSpec(kv_blk, cur), pl.BlockSpec(kv_blk, prev),
                  pl.BlockSpec(kv_blk, cur), pl.BlockSpec(kv_blk, prev),
                  pl.BlockSpec((SWA_Q_HEADS, 2 * SWA_BLOCK), lambda b, n: (0, 0)),
                  pl.BlockSpec((SWA_KV_HEADS, rows, 1), lambda b, n: (0, 0, 0))],
        out_specs=pl.BlockSpec((1, SWA_BLOCK, SWA_WIDTH), cur),
        out_shape=jax.ShapeDtypeStruct((B, S, SWA_WIDTH), jnp.bfloat16),
        scratch_shapes=[pltpu.VMEM((SWA_KV_HEADS, rows, 2 * SWA_BLOCK), jnp.float32)],
        compiler_params=pltpu.CompilerParams(vmem_limit_bytes=VMEM_LIMIT,
                                             dimension_semantics=("arbitrary", "arbitrary")),
        name="swa",
    )(qa, ka, ka, va, va, tab_sw, sink_col)


def _moba_kernel(q_ref, k_ref, v_ref, tab_ref, o_ref, bias_ref, sel_ref, m_ref, acc_ref):
    S = q_ref.shape[1]
    nb = S // MOBA_BLOCK
    blk = MOBA_BLOCK
    lane_lo = lax.broadcasted_iota(jnp.int32, (1, LANES), 1) < HEAD_DIM
    head_masks = (lane_lo, jnp.logical_not(lane_lo))

    @pl.when(pl.program_id(1) == 0)
    def _():
        for hh in range(2):
            for d in range(nb):
                row = jnp.broadcast_to(tab_ref[hh, d:d + 1, :], (blk, 2 * blk))
                bias_ref[hh, d] = pltpu.roll(row, blk, 1, stride=1, stride_axis=0)[:, :blk]

    kmeans = [jnp.mean(k_ref[0, j * blk:(j + 1) * blk, :].astype(jnp.float32), axis=0, keepdims=True)
              for j in range(nb)]
    kmean = jnp.concatenate(kmeans, axis=0)
    causal_t = (lax.broadcasted_iota(jnp.int32, (blk, blk), 0)
                <= lax.broadcasted_iota(jnp.int32, (blk, blk), 1))
    blk_id = lax.broadcasted_iota(jnp.int32, (nb, blk), 0)

    def rows(ref, b):
        return ref[0, pl.ds(pl.multiple_of(b * blk, blk), blk), :]

    def masked_q(i, hh):
        q = rows(q_ref, i)
        return jnp.where(head_masks[hh], q, jnp.zeros_like(q))

    def masked_v(j, hh):
        v = rows(v_ref, j)
        return jnp.where(head_masks[hh], v, jnp.ones_like(v))

    def own_blocks(q_blocks):
        chains = [(i, hh) for i in q_blocks for hh in range(2)]
        qms = [masked_q(i, hh) for i, hh in chains]
        gates = [lax.dot_general(kmean, qm.astype(jnp.float32), _NT, precision=_HIGHEST,
                                 preferred_element_type=jnp.float32) for qm in qms]
        scores = [lax.dot_general(rows(k_ref, i), qm, _NT, preferred_element_type=jnp.float32)
                  for (i, hh), qm in zip(chains, qms)]
        sels = []
        for (i, hh), gate in zip(chains, gates):
            rank = jnp.zeros((nb, blk), jnp.float32)
            for jp in range(nb):
                gj = gate[jp:jp + 1, :]
                beats = jnp.where(gj > gate, 1.0, jnp.where((gj == gate) & (jp < blk_id), 1.0, 0.0))
                rank = rank + jnp.where(jp < i, beats, 0.0)
            sels.append(jnp.where(rank < MOBA_TOPK, 1.0, 0.0))
        ms, ps = [], []
        for (i, hh), s in zip(chains, scores):
            s = s + bias_ref[hh, 0]
            s = jnp.where(causal_t, s, NEG_INF)
            m = jnp.max(s, axis=0, keepdims=True)
            ps.append(jnp.exp(s - m).astype(jnp.bfloat16))
            ms.append(m)
        accs = [lax.dot_general(masked_v(i, hh), p, _TN, preferred_element_type=jnp.float32)
                for (i, hh), p in zip(chains, ps)]
        for (i, hh), sel, m, acc in zip(chains, sels, ms, accs):
            sel_ref[2 * i + hh] = sel
            m_ref[2 * i + hh] = jnp.broadcast_to(m, (SUBLANES, blk))
            acc_ref[2 * i + hh] = acc

    def past_blocks(d, q_blocks):
        chains = [(i, hh) for i in q_blocks for hh in range(2)]
        scores = [lax.dot_general(rows(k_ref, i - d), masked_q(i, hh), _NT, preferred_element_type=jnp.float32)
                  for i, hh in chains]
        ms, ps, alphas = [], [], []
        for (i, hh), s in zip(chains, scores):
            s = s + bias_ref[hh, d]
            picked = jnp.sum(jnp.where(blk_id == i - d, sel_ref[2 * i + hh], 0.0), axis=0, keepdims=True) > 0.5
            s = jnp.where(picked, s, NEG_INF)
            m = m_ref[2 * i + hh][0:1]
            m_new = jnp.maximum(m, jnp.max(s, axis=0, keepdims=True))
            ps.append(jnp.exp(s - m_new).astype(jnp.bfloat16))
            alphas.append(jnp.exp(m - m_new))
            ms.append(m_new)
        pvs = [lax.dot_general(masked_v(i - d, hh), p, _TN, preferred_element_type=jnp.float32)
               for (i, hh), p in zip(chains, ps)]
        accs = [alpha * acc_ref[2 * i + hh] + pv for (i, hh), alpha, pv in zip(chains, alphas, pvs)]
        for (i, hh), m_new, acc in zip(chains, ms, accs):
            m_ref[2 * i + hh] = jnp.broadcast_to(m_new, (SUBLANES, blk))
            acc_ref[2 * i + hh] = acc

    def own_pair(t, carry):
        own_blocks((2 * t, 2 * t + 1))
        return carry

    lax.fori_loop(0, nb // 2, own_pair, 0)

    def diagonal(d, carry):
        def pair(t, c):
            past_blocks(d, (d + 2 * t, d + 2 * t + 1))
            return c

        lax.fori_loop(0, (nb - d) // 2, pair, 0)

        @pl.when((nb - d) % 2 == 1)
        def _():
            past_blocks(d, (nb - 1,))

        return carry

    lax.fori_loop(1, nb, diagonal, 0)

    def finish(i, carry):
        a0 = acc_ref[2 * i]
        a1 = acc_ref[2 * i + 1]
        out_t = jnp.concatenate([a0[:HEAD_DIM] / a0[HEAD_DIM:], a1[HEAD_DIM:] / a1[:HEAD_DIM]], axis=0)
        o_ref[0, pl.ds(pl.multiple_of(i * blk, blk), blk), :] = out_t.T.astype(o_ref.dtype)
        return carry

    lax.fori_loop(0, nb, finish, 0)


def _moba_call(qb, kb, vb, tab_mb):
    B, S, _ = qb.shape
    nb = S // MOBA_BLOCK
    pairs = MOBA_HEADS // 2
    slab = pl.BlockSpec((1, S, LANES), lambda hp, b: (b, 0, hp))
    return pl.pallas_call(
        _moba_kernel,
        grid=(pairs, B),
        in_specs=[slab, slab, slab,
                  pl.BlockSpec((2, nb, 2 * MOBA_BLOCK), lambda hp, b: (hp, 0, 0))],
        out_specs=slab,
        out_shape=jax.ShapeDtypeStruct((B, S, MOBA_WIDTH), jnp.bfloat16),
        scratch_shapes=[pltpu.VMEM((2, nb, MOBA_BLOCK, MOBA_BLOCK), jnp.float32),
                        pltpu.VMEM((2 * nb, nb, MOBA_BLOCK), jnp.float32),
                        pltpu.VMEM((2 * nb, SUBLANES, MOBA_BLOCK), jnp.float32),
                        pltpu.VMEM((2 * nb, LANES, MOBA_BLOCK), jnp.float32)],
        compiler_params=pltpu.CompilerParams(vmem_limit_bytes=VMEM_LIMIT,
                                             dimension_semantics=("arbitrary", "arbitrary")),
        name="moba",
    )(qb, kb, vb, tab_mb)


def _outproj_kernel(ya_ref, yb_ref, x_ref, ga_ref, gb_ref, w_ref, g1_ref, n2_ref, sh_ref, sc_ref, wr_hi_ref, wr_lo_ref,
                    x1_ref, h2p_ref, h2b_ref, lg_ref):
    tm = x_ref.shape[0]

    def norm(y, g):
        y = y.astype(jnp.float32)
        return y * lax.rsqrt(jnp.mean(y * y, axis=-1, keepdims=True) + EPS) * g

    yn = jnp.concatenate([norm(ya_ref[...], ga_ref[...]), norm(yb_ref[...], gb_ref[...])], axis=-1)
    y = jnp.dot(yn.astype(jnp.bfloat16), w_ref[...], preferred_element_type=jnp.float32)
    x1 = x_ref[...] + g1_ref[0] * y
    x1_ref[...] = x1
    h2 = x1 * lax.rsqrt(jnp.mean(x1 * x1, axis=-1, keepdims=True) + EPS) * n2_ref[...]
    h2 = h2 * (1.0 + sc_ref[0]) + sh_ref[0]
    h2b = h2.astype(jnp.bfloat16)
    h2b_ref[...] = h2b
    bits = lax.bitcast_convert_type(h2b.astype(jnp.float32), jnp.int32)
    word = (bits[:, HALF:] & _HI_MASK) | lax.shift_right_logical(bits[:, :HALF], 16)
    for j in range(PACK_CHUNKS):
        h2p_ref[pl.ds(j, tm, stride=PACK_CHUNKS), :] = word[:, j * LANES:(j + 1) * LANES]
    h2_lo = (h2 - h2b.astype(jnp.float32)).astype(jnp.bfloat16)
    wr_hi = wr_hi_ref[...]
    lg = lax.dot_general(wr_hi, h2b, _NT, preferred_element_type=jnp.float32)
    lg = lg + lax.dot_general(wr_hi, h2_lo, _NT, preferred_element_type=jnp.float32)
    lg_ref[...] = lg + lax.dot_general(wr_lo_ref[...], h2b, _NT, preferred_element_type=jnp.float32)


def _outproj_call(ya, yb, x2, ga, gb, w_bf16, g1, n2, sh2, sc2, wr_hi, wr_lo, seq):
    n_tok, D = x2.shape
    tm = OUTPROJ_TM
    per_seq = seq // tm
    row = lambda i: (i, 0)
    const = lambda i: (0, 0)
    mod_spec = pl.BlockSpec((1, 1, D), lambda i: (i // per_seq, 0, 0))
    return pl.pallas_call(
        _outproj_kernel,
        grid=(n_tok // tm,),
        in_specs=[pl.BlockSpec((tm, SWA_WIDTH), row), pl.BlockSpec((tm, MOBA_WIDTH), row),
                  pl.BlockSpec((tm, D), row),
                  pl.BlockSpec((1, SWA_WIDTH), const), pl.BlockSpec((1, MOBA_WIDTH), const),
                  pl.BlockSpec((MIX_WIDTH, D), const),
                  mod_spec, pl.BlockSpec((1, D), const), mod_spec, mod_spec,
                  pl.BlockSpec((N_EXPERTS, D), const), pl.BlockSpec((N_EXPERTS, D), const)],
        out_specs=[pl.BlockSpec((tm, D), row),
                   pl.BlockSpec((tm * PACK_CHUNKS, LANES), row),
                   pl.BlockSpec((tm, D), row),
                   pl.BlockSpec((N_EXPERTS, tm), lambda i: (0, i))],
        out_shape=[jax.ShapeDtypeStruct((n_tok, D), jnp.float32),
                   jax.ShapeDtypeStruct((n_tok * PACK_CHUNKS, LANES), jnp.int32),
                   jax.ShapeDtypeStruct((n_tok, D), jnp.bfloat16),
                   jax.ShapeDtypeStruct((N_EXPERTS, n_tok), jnp.float32)],
        compiler_params=pltpu.CompilerParams(vmem_limit_bytes=VMEM_LIMIT),
        name="outproj",
    )(ya, yb, x2, ga, gb, w_bf16, g1, n2, sh2, sc2, wr_hi, wr_lo)


def _route_kernel(lg_ref, eb_ref, eidx_ref, w_ref, cnt_ref, run_ref):
    tn = lg_ref.shape[1]

    @pl.when(pl.program_id(0) == 0)
    def _():
        run_ref[...] = jnp.zeros_like(run_ref)

    scores = 1.0 / (1.0 + jnp.exp(-lg_ref[...]))
    sel = scores + eb_ref[...]
    neg = -jnp.inf
    g_iota = lax.broadcasted_iota(jnp.int32, (GROUP_SIZE, tn), 0)
    gs = []
    for g in range(N_GROUPS):
        blk = sel[g * GROUP_SIZE:(g + 1) * GROUP_SIZE, :]
        m1 = jnp.max(blk, axis=0, keepdims=True)
        i1 = jnp.min(jnp.where(blk == m1, g_iota, GROUP_SIZE), axis=0, keepdims=True)
        m2 = jnp.max(jnp.where(g_iota == i1, neg, blk), axis=0, keepdims=True)
        gs.append(m1 + m2)
    masked = []
    for g in range(N_GROUPS):
        rank = jnp.zeros((1, tn), jnp.float32)
        for gp in range(N_GROUPS):
            if gp == g:
                continue
            beats = (gs[gp] > gs[g]) | ((gs[gp] == gs[g]) & (gp < g))
            rank = rank + jnp.where(beats, 1.0, 0.0)
        keep = rank < TOPK_GROUPS
        masked.append(jnp.where(keep, sel[g * GROUP_SIZE:(g + 1) * GROUP_SIZE, :], neg))
    masked = jnp.concatenate(masked, axis=0)
    e_iota = lax.broadcasted_iota(jnp.int32, (N_EXPERTS, tn), 0)
    idxs, ws, hits = [], [], []
    for _ in range(TOP_K):
        m = jnp.max(masked, axis=0, keepdims=True)
        idx = jnp.min(jnp.where(masked == m, e_iota, N_EXPERTS), axis=0, keepdims=True)
        hit = e_iota == idx
        ws.append(jnp.sum(jnp.where(hit, scores, 0.0), axis=0, keepdims=True))
        masked = jnp.where(hit, neg, masked)
        idxs.append(idx)
        hits.append(hit)
    wsum = ws[0]
    for k in range(1, TOP_K):
        wsum = wsum + ws[k]
    eidx_ref[...] = jnp.concatenate(idxs, axis=0)
    w_ref[...] = jnp.concatenate(ws, axis=0) / wsum * ROUTED_SCALE
    member = jnp.zeros((N_EXPERTS, tn), jnp.float32)
    for hit in hits:
        member = member + jnp.where(hit, 1.0, 0.0)
    total = run_ref[...] + jnp.sum(member, axis=1, keepdims=True)
    run_ref[...] = total
    cnt_ref[...] = total.astype(jnp.int32)


def _route_call(lg_t, e_bias):
    n_tok = lg_t.shape[1]
    tn = ROUTE_TN
    tok_blk = pl.BlockSpec((TOP_K, tn), lambda i: (0, i))
    return pl.pallas_call(
        _route_kernel,
        grid=(n_tok // tn,),
        in_specs=[pl.BlockSpec((N_EXPERTS, tn), lambda i: (0, i)),
                  pl.BlockSpec((N_EXPERTS, 1), lambda i: (0, 0))],
        out_specs=[tok_blk, tok_blk, pl.BlockSpec((N_EXPERTS, 1), lambda i: (0, 0))],
        out_shape=[jax.ShapeDtypeStruct((TOP_K, n_tok), jnp.int32),
                   jax.ShapeDtypeStruct((TOP_K, n_tok), jnp.float32),
                   jax.ShapeDtypeStruct((N_EXPERTS, 1), jnp.int32)],
        scratch_shapes=[pltpu.VMEM((N_EXPERTS, 1), jnp.float32)],
        compiler_params=pltpu.CompilerParams(vmem_limit_bytes=VMEM_LIMIT,
                                             dimension_semantics=("arbitrary",)),
        name="route",
    )(lg_t, e_bias.reshape(N_EXPERTS, 1))


def _moe_weight_copies(w_hbm, wf, sem, e, slot):
    return [pltpu.make_async_copy(w.at[e], f.at[slot], sem.at[slot]) for w, f in zip(w_hbm, wf)]


def _moe_row_copy(ybuf, yk_hbm, osem, slot, r, dst_row):
    return pltpu.make_async_copy(ybuf.at[slot, pl.ds(r * PACK_CHUNKS, PACK_CHUNKS), :],
                                 yk_hbm.at[pl.ds(pl.multiple_of(dst_row, PACK_CHUNKS), PACK_CHUNKS), :],
                                 osem.at[slot])


def _moe_kernel(blk_e_ref, first_ref, par_ref, next_e_ref, nvalid_ref, tok0_ref, tokn_ref, dstp_ref, h2p_ref,
                w1_hbm, w3_hbm, w2_hbm, yk_hbm, xt, ybuf, wf1, wf3, wf2, wb1, wb3, wb2, sem, osem):
    i = pl.program_id(0)
    nvalid = nvalid_ref[0]
    stride = MOE_XT_STRIDE
    w_hbm = (w1_hbm, w3_hbm, w2_hbm)
    wf = (wf1, wf3, wf2)
    xslot = i % 2
    oslot = i % MOE_OUT_SLOTS
    pslot = (i + MOE_OUT_SLOTS - 1) % MOE_OUT_SLOTS

    def gather_rows(tok_ref, slot):
        for r in range(MOE_TM):
            src = pl.multiple_of(tok_ref[0, 0, r], PACK_CHUNKS)
            xt[slot, pl.ds(r, PACK_CHUNKS, stride=stride), :] = h2p_ref[pl.ds(src, PACK_CHUNKS), :]

    def send_prev_rows():
        for r in range(MOE_TM):
            _moe_row_copy(ybuf, yk_hbm, osem, pslot, r, dstp_ref[0, 0, r]).start(priority=r % 2)

    @pl.when(i == 0)
    def _():
        ybuf[...] = jnp.zeros_like(ybuf)
        blk_rows = MOE_TM * PACK_CHUNKS
        for s in range(MOE_OUT_SLOTS):
            cp = pltpu.make_async_copy(
                ybuf.at[s], yk_hbm.at[pl.ds(yk_hbm.shape[0] - (MOE_OUT_SLOTS - s) * blk_rows, blk_rows), :],
                osem.at[s])
            cp.start()
            cp.wait()
        gather_rows(tok0_ref, 0)
        for cp in _moe_weight_copies(w_hbm, wf, sem, blk_e_ref[0], 0):
            cp.start()

    @pl.when((i >= 2) & (i <= nvalid + 2))
    def _():
        for r in range(MOE_TM):
            _moe_row_copy(ybuf, yk_hbm, osem, oslot, r, 0).wait()

    @pl.when(i == nvalid)
    def _():
        send_prev_rows()

    @pl.when(i < nvalid)
    def _():
        @pl.when(first_ref[i] == 1)
        def _():
            slot = par_ref[i]
            for cp in _moe_weight_copies(w_hbm, wf, sem, blk_e_ref[i], slot):
                cp.wait()

            @pl.when(next_e_ref[i] >= 0)
            def _():
                for cp in _moe_weight_copies(w_hbm, wf, sem, next_e_ref[i], 1 - slot):
                    cp.start()

            wb1[...] = wf1[slot].astype(jnp.bfloat16)
            wb3[...] = wf3[slot].astype(jnp.bfloat16)
            wb2[...] = wf2[slot].astype(jnp.bfloat16)

        lo, hi = [], []
        for j in range(PACK_CHUNKS):
            word = xt[xslot, pl.ds(j * stride, MOE_TM), :]
            lo.append(lax.bitcast_convert_type(lax.shift_left(word, 16), jnp.float32).astype(jnp.bfloat16))
            hi.append(lax.bitcast_convert_type(word & _HI_MASK, jnp.float32).astype(jnp.bfloat16))
        x = jnp.concatenate(lo + hi, axis=-1)
        a = jnp.dot(x, wb1[...], preferred_element_type=jnp.float32)
        gather_rows(tokn_ref, 1 - xslot)
        b = jnp.dot(x, wb3[...], preferred_element_type=jnp.float32)
        h = (_silu(a) * b).astype(jnp.bfloat16)
        send_prev_rows()
        y = jnp.dot(h, wb2[...], preferred_element_type=jnp.float32)
        bits = lax.bitcast_convert_type(y.astype(jnp.bfloat16).astype(jnp.float32), jnp.int32)
        word = (bits[:, HALF:] & _HI_MASK) | lax.shift_right_logical(bits[:, :HALF], 16)
        for j in range(PACK_CHUNKS):
            ybuf[oslot, pl.ds(j, MOE_TM, stride=PACK_CHUNKS), :] = word[:, j * LANES:(j + 1) * LANES]


def _moe_call(blk_e, first, parity, next_e, nvalid, buf_tok3, dst3, h2p, w1, w3, w2, n_rows):
    nblk = buf_tok3.shape[0]
    D = D_MODEL
    hbm = pl.BlockSpec(memory_space=pl.ANY)

    def blk_ids(shift, n):
        return pl.BlockSpec((1, 1, MOE_TM), lambda i, *_: (jnp.minimum(i + shift, n - 1), 0, 0),
                            memory_space=pltpu.SMEM)

    grid_spec = pltpu.PrefetchScalarGridSpec(
        num_scalar_prefetch=5,
        grid=(nblk + MOE_OUT_SLOTS - 1,),
        in_specs=[blk_ids(0, nblk), blk_ids(1, nblk), blk_ids(0, nblk + 1),
                  pl.BlockSpec(h2p.shape, lambda i, *_: (0, 0), pipeline_mode=pl.Buffered(1)),
                  hbm, hbm, hbm],
        out_specs=hbm,
        scratch_shapes=[pltpu.VMEM((2, PACK_CHUNKS * MOE_XT_STRIDE, LANES), jnp.int32),
                        pltpu.VMEM((MOE_OUT_SLOTS, MOE_TM * PACK_CHUNKS, LANES), jnp.int32),
                        pltpu.VMEM((2, D, EXPERT_DIM), jnp.float32),
                        pltpu.VMEM((2, D, EXPERT_DIM), jnp.float32),
                        pltpu.VMEM((2, EXPERT_DIM, D), jnp.float32),
                        pltpu.VMEM((D, EXPERT_DIM), jnp.bfloat16),
                        pltpu.VMEM((D, EXPERT_DIM), jnp.bfloat16),
                        pltpu.VMEM((EXPERT_DIM, D), jnp.bfloat16),
                        pltpu.SemaphoreType.DMA((2,)),
                        pltpu.SemaphoreType.DMA((MOE_OUT_SLOTS,))],
    )
    return pl.pallas_call(
        _moe_kernel,
        grid_spec=grid_spec,
        out_shape=jax.ShapeDtypeStruct((n_rows * PACK_CHUNKS, LANES), jnp.int32),
        compiler_params=pltpu.CompilerParams(vmem_limit_bytes=MOE_VMEM_LIMIT,
                                             dimension_semantics=("arbitrary",)),
        name="moe",
    )(blk_e, first, parity, next_e, nvalid, buf_tok3, buf_tok3, dst3, h2p, w1, w3, w2)


def _combine_kernel(yk_ref, wk_ref, x1_ref, h2_ref, g2_ref, ws1_ref, ws3_ref, ws2_ref, fg_ref, o_ref):
    tm = COMBINE_TM
    h2 = h2_ref[...]
    a = jnp.dot(h2, ws1_ref[...], preferred_element_type=jnp.float32)
    b = jnp.dot(h2, ws3_ref[...], preferred_element_type=jnp.float32)
    acc = jnp.dot((_silu(a) * b).astype(jnp.bfloat16), ws2_ref[...], preferred_element_type=jnp.float32)
    wk = wk_ref[...]
    for k in range(TOP_K):
        lo, hi = [], []
        for j in range(PACK_CHUNKS):
            word = yk_ref[pl.ds(k * tm * PACK_CHUNKS + j, tm, stride=PACK_CHUNKS), :]
            lo.append(lax.bitcast_convert_type(lax.shift_left(word, 16), jnp.float32))
            hi.append(lax.bitcast_convert_type(word & _HI_MASK, jnp.float32))
        acc = acc + wk[:, k:k + 1] * jnp.concatenate(lo + hi, axis=-1)
    x = x1_ref[...] + g2_ref[0] * acc
    o_ref[...] = x * lax.rsqrt(jnp.mean(x * x, axis=-1, keepdims=True) + EPS) * fg_ref[...]


def _combine_call(yk, wk, x1, h2b, g2, ws1, ws3, ws2, fg, seq):
    n_tok, D = x1.shape
    tm = COMBINE_TM
    per_seq = seq // tm
    row = lambda i: (i, 0)
    const = lambda i: (0, 0)
    n_steps = n_tok // tm
    return pl.pallas_call(
        _combine_kernel,
        grid=(n_steps,),
        in_specs=[pl.BlockSpec((TOP_K * tm * PACK_CHUNKS, LANES), row),
                  pl.BlockSpec((tm, TOP_K), row),
                  pl.BlockSpec((tm, D), row),
                  pl.BlockSpec((tm, D), row),
                  pl.BlockSpec((1, 1, D), lambda i: (i // per_seq, 0, 0)),
                  pl.BlockSpec((D, SHARED_DIM), const),
                  pl.BlockSpec((D, SHARED_DIM), const),
                  pl.BlockSpec((SHARED_DIM, D), const),
                  pl.BlockSpec((1, D), const)],
        out_specs=pl.BlockSpec((tm, D), row),
        out_shape=jax.ShapeDtypeStruct((n_tok, D), jnp.float32),
        compiler_params=pltpu.CompilerParams(vmem_limit_bytes=VMEM_LIMIT),
        name="combine",
    )(yk, wk, x1, h2b, g2, ws1, ws3, ws2, fg)


def _pair_heads(a, axis):
    shape = a.shape
    a = a.reshape(shape[:axis] + (SWA_KV_HEADS, SWA_GROUP, HEAD_DIM) + shape[axis + 1:])
    a = jnp.swapaxes(a, axis, axis + 1)
    return a.reshape(shape)


def _bias_vectors(rel_table, seq):
    vec = rel_table.astype(jnp.float32)[_t5_bucket(jnp.arange(seq, dtype=jnp.int32))]
    m = jnp.arange(2 * SWA_BLOCK)
    d_a = jnp.clip(((-m) % (2 * SWA_BLOCK)) - SWA_BLOCK, 0, seq - 1)
    tab_sw = vec[d_a][:, :SWA_Q_HEADS].T
    nb = seq // MOBA_BLOCK
    d_b = jnp.clip(jnp.arange(nb)[:, None] * MOBA_BLOCK - MOBA_BLOCK + jnp.arange(2 * MOBA_BLOCK)[None, :],
                   0, seq - 1)
    tab_mb = jnp.transpose(vec[d_b][..., SWA_Q_HEADS:], (2, 0, 1))
    return tab_sw, tab_mb


def _route_plan(eidx_t, counts, n_tok):
    A = n_tok * TOP_K
    P = A + N_EXPERTS * MOE_TM
    nblk = P // MOE_TM
    counts = counts.reshape(N_EXPERTS)
    pcounts = (counts + MOE_TM - 1) // MOE_TM * MOE_TM
    pends = jnp.cumsum(pcounts)
    pstarts = pends - pcounts
    blk_start = jnp.arange(nblk, dtype=jnp.int32) * MOE_TM
    blk_e = jnp.minimum(jnp.sum((pends[None, :] <= blk_start[:, None]).astype(jnp.int32), axis=1),
                        N_EXPERTS - 1).astype(jnp.int32)
    nvalid = (pends[-1] // MOE_TM).astype(jnp.int32).reshape(1)
    first = jnp.concatenate([jnp.ones((1,), jnp.int32), (blk_e[1:] != blk_e[:-1]).astype(jnp.int32)])
    parity = (jnp.cumsum(first) - 1) & 1
    run_end = pends[blk_e] // MOE_TM
    next_e = jnp.where(run_end < nvalid[0], blk_e[jnp.minimum(run_end, nblk - 1)], -1).astype(jnp.int32)
    slot = (jnp.arange(n_tok, dtype=jnp.int32) * TOP_K)[None, :] + jnp.arange(TOP_K, dtype=jnp.int32)[:, None]
    real_keys = (eidx_t << KEY_SHIFT) | slot
    s = jnp.arange(MOE_TM, dtype=jnp.int32)[None, :]
    e = jnp.arange(N_EXPERTS, dtype=jnp.int32)[:, None]
    pad_keys = jnp.where(s < (pcounts - counts)[:, None], (e << KEY_SHIFT) | (A + s),
                         (N_EXPERTS << KEY_SHIFT) | (A + s))
    keys = jnp.sort(jnp.concatenate([real_keys.reshape(A), pad_keys.reshape(N_EXPERTS * MOE_TM)]))
    slot_sorted = keys & ((1 << KEY_SHIFT) - 1)
    real = slot_sorted < A
    tok = slot_sorted // TOP_K
    buf_tok = jnp.where(real, tok, 0).astype(jnp.int32)
    pick = slot_sorted % TOP_K
    dst = ((tok // COMBINE_TM) * TOP_K + pick) * COMBINE_TM + tok % COMBINE_TM
    row = jnp.arange(P, dtype=jnp.int32)
    spare = A + ((row // MOE_TM) % 2) * MOE_TM + row % MOE_TM
    dst = jnp.where(real, dst, spare).astype(jnp.int32)
    dummy = A + 2 * MOE_TM + jnp.arange(MOE_TM, dtype=jnp.int32)
    dst = jnp.concatenate([dummy, dst]) * PACK_CHUNKS
    blocks = (blk_e, first, parity.astype(jnp.int32), next_e, nvalid)
    return ((buf_tok * PACK_CHUNKS).reshape(nblk, 1, MOE_TM), dst.reshape(nblk + 1, 1, MOE_TM), blocks)


def kernel(x, c, w_ada, b_ada, norm1_g, w_in, sinks, rel_table, out_norm_a, out_norm_b, w_out, norm2_g,
           w_router, e_bias, w1, w3, w2, ws1, ws3, ws2, final_g):
    B, S, D = x.shape
    assert w_ada.shape[0] == 1, "single-layer stack only"
    assert D == D_MODEL and S % INPROJ_TM == 0 and S % MOBA_BLOCK == 0
    assert B * S * TOP_K + MOE_TM <= (1 << KEY_SHIFT)
    n_tok = B * S
    bf16 = jnp.bfloat16
    drop = lambda a: a.reshape(a.shape[1:])
    tab_sw, tab_mb = _bias_vectors(rel_table, S)
    x2 = x.reshape(n_tok, D)

    mod = _ada_call(c, drop(w_ada), drop(b_ada))
    sh1, sc1, g1, sh2, sc2, g2 = [m.reshape(B, 1, D) for m in jnp.split(mod, 6, axis=-1)]
    w_in2 = drop(w_in)
    k_cols = jnp.concatenate([jnp.ones((SWA_WIDTH,)), jnp.full((SWA_KV_WIDTH,), ATTN_SCALE),
                              jnp.ones((SWA_KV_WIDTH + MOBA_WIDTH,)), jnp.full((MOBA_WIDTH,), ATTN_SCALE),
                              jnp.ones((MOBA_WIDTH,))]).astype(jnp.float32)
    w_in_p = (jnp.concatenate([_pair_heads(w_in2[:, :SWA_WIDTH], 1), w_in2[:, SWA_WIDTH:]], axis=1)
              * k_cols[None, :]).astype(bf16)
    qa, ka, va, qb, kb, vb = _inproj_call(x2, norm1_g.reshape(1, D), sh1, sc1, w_in_p, S)
    sink_col = jnp.repeat(sinks.reshape(SWA_Q_HEADS).astype(jnp.float32), SWA_BLOCK).reshape(
        SWA_KV_HEADS, SWA_GROUP * SWA_BLOCK, 1)
    ya = _swa_call(qa.reshape(B, S, SWA_WIDTH), ka.reshape(B, S, SWA_KV_WIDTH),
                   va.reshape(B, S, SWA_KV_WIDTH), tab_sw, sink_col)
    yb = _moba_call(qb.reshape(B, S, MOBA_WIDTH), kb.reshape(B, S, MOBA_WIDTH),
                    vb.reshape(B, S, MOBA_WIDTH), tab_mb)
    w_rt = drop(w_router).T
    wr_hi = w_rt.astype(bf16)
    wr_lo = (w_rt - wr_hi.astype(jnp.float32)).astype(bf16)
    w_out2 = drop(w_out)
    w_out_p = jnp.concatenate([_pair_heads(w_out2[:SWA_WIDTH], 0), w_out2[SWA_WIDTH:]], axis=0).astype(bf16)
    x1, h2p, h2b, lg_t = _outproj_call(
        ya.reshape(n_tok, SWA_WIDTH), yb.reshape(n_tok, MOBA_WIDTH), x2,
        _pair_heads(out_norm_a.reshape(SWA_WIDTH), 0).reshape(1, SWA_WIDTH), out_norm_b.reshape(1, MOBA_WIDTH),
        w_out_p, g1, norm2_g.reshape(1, D), sh2, sc2, wr_hi, wr_lo, S)
    eidx_t, w_t, counts = _route_call(lg_t, e_bias.reshape(N_EXPERTS))
    buf_tok3, dst3, blocks = _route_plan(eidx_t, counts, n_tok)
    yk = _moe_call(*blocks, buf_tok3, dst3, h2p, drop(w1), drop(w3), drop(w2),
                   n_tok * TOP_K + MOE_OUT_SLOTS * MOE_TM)
    out = _combine_call(yk, w_t.T, x1, h2b, g2, drop(ws1).astype(bf16), drop(ws3).astype(bf16),
                        drop(ws2).astype(bf16), final_g.reshape(1, D), S)
    return out.reshape(B, S, D)
```

```python
import math

import jax
import jax.numpy as jnp
from jax import lax
from jax.experimental import pallas as pl
from jax.experimental.pallas import tpu as pltpu

D_MODEL = 1024
HEAD_DIM = 64
SWA_Q_HEADS = 8
SWA_KV_HEADS = 2
SWA_GROUP = SWA_Q_HEADS // SWA_KV_HEADS
SWA_WINDOW = 128
SWA_BLOCK = 128
SWA_WIDTH = SWA_Q_HEADS * HEAD_DIM
SWA_KV_WIDTH = SWA_KV_HEADS * HEAD_DIM
MOBA_HEADS = 8
MOBA_BLOCK = 256
MOBA_TOPK = 3
MOBA_WIDTH = MOBA_HEADS * HEAD_DIM
N_HEADS = SWA_Q_HEADS + MOBA_HEADS
MIX_WIDTH = SWA_WIDTH + MOBA_WIDTH
IN_COLS = SWA_WIDTH + 2 * SWA_KV_WIDTH + 3 * MOBA_WIDTH
ATTN_SCALE = HEAD_DIM ** -0.5
REL_BUCKETS = 32
REL_MAX_DIST = 1024
N_EXPERTS = 256
TOP_K = 8
N_GROUPS = 8
TOPK_GROUPS = 4
GROUP_SIZE = N_EXPERTS // N_GROUPS
EXPERT_DIM = 256
SHARED_DIM = 256
ROUTED_SCALE = 2.5
EPS = 1e-6
NEG_INF = -1e30

LANES = 128
SUBLANES = 8
ROW_CHUNKS = D_MODEL // LANES
HALF = D_MODEL // 2
PACK_CHUNKS = HALF // LANES

ADA_TN = 512
INPROJ_TM = 512
OUTPROJ_TM = 256
ROUTE_TN = 512
MOE_TM = 256
MOE_XT_STRIDE = MOE_TM + SUBLANES
MOE_OUT_SLOTS = 3
MOE_W_SLOTS = 3
COMBINE_TM = 128
KEY_SHIFT = 18
VMEM_LIMIT = 48 * 1024 * 1024
MOE_VMEM_LIMIT = 58 * 1024 * 1024

_HIGHEST = lax.Precision.HIGHEST
_NT = (((1,), (1,)), ((), ()))
_TN = (((0,), (0,)), ((), ()))
_HI_MASK = -65536


def _silu(a):
    return a * (1.0 / (1.0 + jnp.exp(-a)))


def _t5_bucket(dist):
    n = jnp.maximum(dist, 0)
    max_exact = REL_BUCKETS // 2
    nf = jnp.maximum(n, 1).astype(jnp.float32)
    large = max_exact + (jnp.log(nf / max_exact) / math.log(REL_MAX_DIST / max_exact)
                         * (REL_BUCKETS - max_exact)).astype(jnp.int32)
    large = jnp.minimum(large, REL_BUCKETS - 1)
    return jnp.where(n < max_exact, n, large)


def _ada_kernel(c_ref, w_ref, b_ref, o_ref):
    c = c_ref[...]
    o_ref[...] = jnp.dot(_silu(c), w_ref[...], precision=_HIGHEST,
                         preferred_element_type=jnp.float32) + b_ref[...]


def _ada_call(c, w_ada, b_ada):
    B, D = c.shape
    n_out = w_ada.shape[1]
    return pl.pallas_call(
        _ada_kernel,
        grid=(n_out // ADA_TN,),
        in_specs=[pl.BlockSpec((B, D), lambda j: (0, 0)),
                  pl.BlockSpec((D, ADA_TN), lambda j: (0, j)),
                  pl.BlockSpec((1, ADA_TN), lambda j: (0, j))],
        out_specs=pl.BlockSpec((B, ADA_TN), lambda j: (0, j)),
        out_shape=jax.ShapeDtypeStruct((B, n_out), jnp.float32),
        name="adaln",
    )(c, w_ada, b_ada.reshape(1, n_out))


def _inproj_kernel(x_ref, g_ref, sh_ref, sc_ref, w_ref, qa_ref, ka_ref, va_ref, qb_ref, kb_ref, vb_ref):
    x = x_ref[...]
    h = x * lax.rsqrt(jnp.mean(x * x, axis=-1, keepdims=True) + EPS) * g_ref[...]
    h = h * (1.0 + sc_ref[0]) + sh_ref[0]
    p = jnp.dot(h.astype(jnp.bfloat16), w_ref[...], preferred_element_type=jnp.float32)
    off = 0
    for ref in (qa_ref, ka_ref, va_ref, qb_ref, kb_ref, vb_ref):
        width = ref.shape[-1]
        ref[...] = p[:, off:off + width].astype(ref.dtype)
        off += width


def _inproj_call(x2, g, sh, sc, w_bf16, seq):
    n_tok, D = x2.shape
    tm = INPROJ_TM
    per_seq = seq // tm
    widths = (SWA_WIDTH, SWA_KV_WIDTH, SWA_KV_WIDTH, MOBA_WIDTH, MOBA_WIDTH, MOBA_WIDTH)
    mod_spec = pl.BlockSpec((1, 1, D), lambda i: (i // per_seq, 0, 0))
    return pl.pallas_call(
        _inproj_kernel,
        grid=(n_tok // tm,),
        in_specs=[pl.BlockSpec((tm, D), lambda i: (i, 0)),
                  pl.BlockSpec((1, D), lambda i: (0, 0)),
                  mod_spec, mod_spec,
                  pl.BlockSpec((D, IN_COLS), lambda i: (0, 0))],
        out_specs=[pl.BlockSpec((tm, w), lambda i: (i, 0)) for w in widths],
        out_shape=[jax.ShapeDtypeStruct((n_tok, w), jnp.bfloat16) for w in widths],
        compiler_params=pltpu.CompilerParams(vmem_limit_bytes=VMEM_LIMIT),
        name="inproj",
    )(x2, g, sh, sc, w_bf16)


def _swa_kernel(q_ref, kc_ref, kp_ref, vc_ref, vp_ref, tab_ref, sink_ref, o_ref, bias_ref):
    n = pl.program_id(1)
    rows = SWA_GROUP * SWA_BLOCK
    lo = lax.broadcasted_iota(jnp.int32, (1, LANES), 1) < HEAD_DIM

    @pl.when((pl.program_id(0) == 0) & (n == 0))
    def _():
        for h in range(SWA_Q_HEADS):
            kvh, c = divmod(h, SWA_GROUP)
            row = jnp.broadcast_to(tab_ref[h:h + 1, :], (SWA_BLOCK, 2 * SWA_BLOCK))
            bias_ref[kvh, c * SWA_BLOCK:(c + 1) * SWA_BLOCK, :] = pltpu.roll(row, 0, 1, stride=1, stride_axis=0)

    kk = jnp.concatenate([kp_ref[0], kc_ref[0]], axis=0)
    vv = jnp.concatenate([vp_ref[0], vc_ref[0]], axis=0)
    qi = lax.broadcasted_iota(jnp.int32, (rows, 2 * SWA_BLOCK), 0) & (SWA_BLOCK - 1)
    kj = lax.broadcasted_iota(jnp.int32, (rows, 2 * SWA_BLOCK), 1)
    dist = qi + SWA_BLOCK - kj
    valid = (dist >= 0) & (dist < SWA_WINDOW) & ((kj >= SWA_BLOCK) | (n > 0))
    qs = jnp.concatenate([q_ref[0, :, c * LANES:(c + 1) * LANES] for c in range(SWA_GROUP)], axis=0)
    masks = (lo, jnp.logical_not(lo))
    scores = [lax.dot_general(jnp.where(hm, qs, jnp.zeros_like(qs)), kk, _NT, preferred_element_type=jnp.float32)
              for hm in masks]
    ps, sink_terms = [], []
    for kvh, s in enumerate(scores):
        s = s + bias_ref[kvh]
        s = jnp.where(valid, s, NEG_INF)
        sink = sink_ref[kvh]
        m = jnp.maximum(jnp.max(s, axis=-1, keepdims=True), sink)
        ps.append(jnp.exp(s - m).astype(jnp.bfloat16))
        sink_terms.append(jnp.exp(sink - m))
    accs = [jnp.dot(p, jnp.where(hm, vv, jnp.ones_like(vv)), preferred_element_type=jnp.float32)
            for hm, p in zip(masks, ps)]
    outs = [acc / (pltpu.roll(acc, HEAD_DIM, 1) + sink_term) for acc, sink_term in zip(accs, sink_terms)]
    for c in range(SWA_GROUP):
        blk = jnp.where(lo, outs[0][c * SWA_BLOCK:(c + 1) * SWA_BLOCK],
                        outs[1][c * SWA_BLOCK:(c + 1) * SWA_BLOCK])
        o_ref[0, :, c * LANES:(c + 1) * LANES] = blk.astype(o_ref.dtype)


def _swa_call(qa, ka, va, tab_sw, sink_col):
    B, S, _ = qa.shape
    nb = S // SWA_BLOCK
    rows = SWA_GROUP * SWA_BLOCK
    cur = lambda b, n: (b, n, 0)
    prev = lambda b, n: (b, jnp.maximum(n - 1, 0), 0)
    kv_blk = (1, SWA_BLOCK, SWA_KV_WIDTH)
    return pl.pallas_call(
        _swa_kernel,
        grid=(B, nb),
        in_specs=[pl.BlockSpec((1, SWA_BLOCK, SWA_WIDTH), cur),
                  pl.BlockSpec(kv_blk, cur), pl.BlockSpec(kv_blk, prev),
                  pl.BlockSpec(kv_blk, cur), pl.BlockSpec(kv_blk, prev),
                  pl.BlockSpec((SWA_Q_HEADS, 2 * SWA_BLOCK), lambda b, n: (0, 0)),
                  pl.BlockSpec((SWA_KV_HEADS, rows, 1), lambda b, n: (0, 0, 0))],
        out_specs=pl.BlockSpec((1, SWA_BLOCK, SWA_WIDTH), cur),
        out_shape=jax.ShapeDtypeStruct((B, S, SWA_WIDTH), jnp.bfloat16),
        scratch_shapes=[pltpu.VMEM((SWA_KV_HEADS, rows, 2 * SWA_BLOCK), jnp.float32)],
        compiler_params=pltpu.CompilerParams(vmem_limit_bytes=VMEM_LIMIT,
                                             dimension_semantics=("arbitrary", "arbitrary")),
        name="swa",
    )(qa, ka, ka, va, va, tab_sw, sink_col)


def _moba_kernel(q_ref, k_ref, v_ref, tab_ref, o_ref, bias_ref, sel_ref, m_ref, acc_ref):
    S = q_ref.shape[1]
    nb = S // MOBA_BLOCK
    blk = MOBA_BLOCK
    lane_lo = lax.broadcasted_iota(jnp.int32, (1, LANES), 1) < HEAD_DIM
    head_masks = (lane_lo, jnp.logical_not(lane_lo))

    @pl.when(pl.program_id(1) == 0)
    def _():
        for hh in range(2):
            for d in range(nb):
                row = jnp.broadcast_to(tab_ref[hh, d:d + 1, :], (blk, 2 * blk))
                bias_ref[hh, d] = pltpu.roll(row, blk, 1, stride=1, stride_axis=0)[:, :blk]

    kmeans = [jnp.mean(k_ref[0, j * blk:(j + 1) * blk, :].astype(jnp.float32), axis=0, keepdims=True)
              for j in range(nb)]
    kmean = jnp.concatenate(kmeans, axis=0)
    causal_t = (lax.broadcasted_iota(jnp.int32, (blk, blk), 0)
                <= lax.broadcasted_iota(jnp.int32, (blk, blk), 1))
    blk_id = lax.broadcasted_iota(jnp.int32, (nb, blk), 0)

    def rows(ref, b):
        return ref[0, pl.ds(pl.multiple_of(b * blk, blk), blk), :]

    def masked_q(i, hh):
        q = rows(q_ref, i)
        return jnp.where(head_masks[hh], q, jnp.zeros_like(q))

    def masked_v(j, hh):
        v = rows(v_ref, j)
        return jnp.where(head_masks[hh], v, jnp.ones_like(v))

    def own_blocks(q_blocks):
        chains = [(i, hh) for i in q_blocks for hh in range(2)]
        qms = [masked_q(i, hh) for i, hh in chains]
        gates = [lax.dot_general(kmean, qm.astype(jnp.float32), _NT, precision=_HIGHEST,
                                 preferred_element_type=jnp.float32) for qm in qms]
        scores = [lax.dot_general(rows(k_ref, i), qm, _NT, preferred_element_type=jnp.float32)
                  for (i, hh), qm in zip(chains, qms)]
        sels = []
        for (i, hh), gate in zip(chains, gates):
            rank = jnp.zeros((nb, blk), jnp.float32)
            for jp in range(nb):
                gj = gate[jp:jp + 1, :]
                beats = jnp.where(gj > gate, 1.0, jnp.where((gj == gate) & (jp < blk_id), 1.0, 0.0))
                rank = rank + jnp.where(jp < i, beats, 0.0)
            sels.append(jnp.where(rank < MOBA_TOPK, 1.0, 0.0))
        ms, ps = [], []
        for (i, hh), s in zip(chains, scores):
            s = s + bias_ref[hh, 0]
            s = jnp.where(causal_t, s, NEG_INF)
            m = jnp.max(s, axis=0, keepdims=True)
            ps.append(jnp.exp(s - m).astype(jnp.bfloat16))
            ms.append(m)
        accs = [lax.dot_general(masked_v(i, hh), p, _TN, preferred_element_type=jnp.float32)
                for (i, hh), p in zip(chains, ps)]
        for (i, hh), sel, m, acc in zip(chains, sels, ms, accs):
            sel_ref[2 * i + hh] = sel
            m_ref[2 * i + hh] = jnp.broadcast_to(m, (SUBLANES, blk))
            acc_ref[2 * i + hh] = acc

    def past_blocks(d, q_blocks):
        chains = [(i, hh) for i in q_blocks for hh in range(2)]
        scores = [lax.dot_general(rows(k_ref, i - d), masked_q(i, hh), _NT, preferred_element_type=jnp.float32)
                  for i, hh in chains]
        ms, ps, alphas = [], [], []
        for (i, hh), s in zip(chains, scores):
            s = s + bias_ref[hh, d]
            picked = jnp.sum(jnp.where(blk_id == i - d, sel_ref[2 * i + hh], 0.0), axis=0, keepdims=True) > 0.5
            s = jnp.where(picked, s, NEG_INF)
            m = m_ref[2 * i + hh][0:1]
            m_new = jnp.maximum(m, jnp.max(s, axis=0, keepdims=True))
            ps.append(jnp.exp(s - m_new).astype(jnp.bfloat16))
            alphas.append(jnp.exp(m - m_new))
            ms.append(m_new)
        pvs = [lax.dot_general(masked_v(i - d, hh), p, _TN, preferred_element_type=jnp.float32)
               for (i, hh), p in zip(chains, ps)]
        accs = [alpha * acc_ref[2 * i + hh] + pv for (i, hh), alpha, pv in zip(chains, alphas, pvs)]
        for (i, hh), m_new, acc in zip(chains, ms, accs):
            m_ref[2 * i + hh] = jnp.broadcast_to(m_new, (SUBLANES, blk))
            acc_ref[2 * i + hh] = acc

    def own_pair(t, carry):
        own_blocks((2 * t, 2 * t + 1))
        return carry

    lax.fori_loop(0, nb // 2, own_pair, 0)

    def diagonal(d, carry):
        def pair(t, c):
            past_blocks(d, (d + 2 * t, d + 2 * t + 1))
            return c

        lax.fori_loop(0, (nb - d) // 2, pair, 0)

        @pl.when((nb - d) % 2 == 1)
        def _():
            past_blocks(d, (nb - 1,))

        return carry

    lax.fori_loop(1, nb, diagonal, 0)

    def finish(i, carry):
        a0 = acc_ref[2 * i]
        a1 = acc_ref[2 * i + 1]
        out_t = jnp.concatenate([a0[:HEAD_DIM] / a0[HEAD_DIM:], a1[HEAD_DIM:] / a1[:HEAD_DIM]], axis=0)
        o_ref[0, pl.ds(pl.multiple_of(i * blk, blk), blk), :] = out_t.T.astype(o_ref.dtype)
        return carry

    lax.fori_loop(0, nb, finish, 0)


def _moba_call(qb, kb, vb, tab_mb):
    B, S, _ = qb.shape
    nb = S // MOBA_BLOCK
    pairs = MOBA_HEADS // 2
    slab = pl.BlockSpec((1, S, LANES), lambda hp, b: (b, 0, hp))
    return pl.pallas_call(
        _moba_kernel,
        grid=(pairs, B),
        in_specs=[slab, slab, slab,
                  pl.BlockSpec((2, nb, 2 * MOBA_BLOCK), lambda hp, b: (hp, 0, 0))],
        out_specs=slab,
        out_shape=jax.ShapeDtypeStruct((B, S, MOBA_WIDTH), jnp.bfloat16),
        scratch_shapes=[pltpu.VMEM((2, nb, MOBA_BLOCK, MOBA_BLOCK), jnp.float32),
                        pltpu.VMEM((2 * nb, nb, MOBA_BLOCK), jnp.float32),
                        pltpu.VMEM((2 * nb, SUBLANES, MOBA_BLOCK), jnp.float32),
                        pltpu.VMEM((2 * nb, LANES, MOBA_BLOCK), jnp.float32)],
        compiler_params=pltpu.CompilerParams(vmem_limit_bytes=VMEM_LIMIT,
                                             dimension_semantics=("arbitrary", "arbitrary")),
        name="moba",
    )(qb, kb, vb, tab_mb)


def _outproj_kernel(ya_ref, yb_ref, x_ref, ga_ref, gb_ref, w_ref, g1_ref, n2_ref, sh_ref, sc_ref, wr_hi_ref, wr_lo_ref,
                    x1_ref, h2p_ref, h2b_ref, lg_ref):
    tm = x_ref.shape[0]

    def norm(y, g):
        y = y.astype(jnp.float32)
        return y * lax.rsqrt(jnp.mean(y * y, axis=-1, keepdims=True) + EPS) * g

    yn = jnp.concatenate([norm(ya_ref[...], ga_ref[...]), norm(yb_ref[...], gb_ref[...])], axis=-1)
    y = jnp.dot(yn.astype(jnp.bfloat16), w_ref[...], preferred_element_type=jnp.float32)
    x1 = x_ref[...] + g1_ref[0] * y
    x1_ref[...] = x1
    h2 = x1 * lax.rsqrt(jnp.mean(x1 * x1, axis=-1, keepdims=True) + EPS) * n2_ref[...]
    h2 = h2 * (1.0 + sc_ref[0]) + sh_ref[0]
    h2b = h2.astype(jnp.bfloat16)
    h2b_ref[...] = h2b
    bits = lax.bitcast_convert_type(h2b.astype(jnp.float32), jnp.int32)
    word = (bits[:, HALF:] & _HI_MASK) | lax.shift_right_logical(bits[:, :HALF], 16)
    for j in range(PACK_CHUNKS):
        h2p_ref[pl.ds(j, tm, stride=PACK_CHUNKS), :] = word[:, j * LANES:(j + 1) * LANES]
    h2_lo = (h2 - h2b.astype(jnp.float32)).astype(jnp.bfloat16)
    wr_hi = wr_hi_ref[...]
    lg = lax.dot_general(wr_hi, h2b, _NT, preferred_element_type=jnp.float32)
    lg = lg + lax.dot_general(wr_hi, h2_lo, _NT, preferred_element_type=jnp.float32)
    lg_ref[...] = lg + lax.dot_general(wr_lo_ref[...], h2b, _NT, preferred_element_type=jnp.float32)


def _outproj_call(ya, yb, x2, ga, gb, w_bf16, g1, n2, sh2, sc2, wr_hi, wr_lo, seq):
    n_tok, D = x2.shape
    tm = OUTPROJ_TM
    per_seq = seq // tm
    row = lambda i: (i, 0)
    const = lambda i: (0, 0)
    mod_spec = pl.BlockSpec((1, 1, D), lambda i: (i // per_seq, 0, 0))
    return pl.pallas_call(
        _outproj_kernel,
        grid=(n_tok // tm,),
        in_specs=[pl.BlockSpec((tm, SWA_WIDTH), row), pl.BlockSpec((tm, MOBA_WIDTH), row),
                  pl.BlockSpec((tm, D), row),
                  pl.BlockSpec((1, SWA_WIDTH), const), pl.BlockSpec((1, MOBA_WIDTH), const),
                  pl.BlockSpec((MIX_WIDTH, D), const),
                  mod_spec, pl.BlockSpec((1, D), const), mod_spec, mod_spec,
                  pl.BlockSpec((N_EXPERTS, D), const), pl.BlockSpec((N_EXPERTS, D), const)],
        out_specs=[pl.BlockSpec((tm, D), row),
                   pl.BlockSpec((tm * PACK_CHUNKS, LANES), row),
                   pl.BlockSpec((tm, D), row),
                   pl.BlockSpec((N_EXPERTS, tm), lambda i: (0, i))],
        out_shape=[jax.ShapeDtypeStruct((n_tok, D), jnp.float32),
                   jax.ShapeDtypeStruct((n_tok * PACK_CHUNKS, LANES), jnp.int32),
                   jax.ShapeDtypeStruct((n_tok, D), jnp.bfloat16),
                   jax.ShapeDtypeStruct((N_EXPERTS, n_tok), jnp.float32)],
        compiler_params=pltpu.CompilerParams(vmem_limit_bytes=VMEM_LIMIT),
        name="outproj",
    )(ya, yb, x2, ga, gb, w_bf16, g1, n2, sh2, sc2, wr_hi, wr_lo)


def _route_kernel(lg_ref, eb_ref, eidx_ref, w_ref, cnt_ref, run_ref):
    tn = lg_ref.shape[1]

    @pl.when(pl.program_id(0) == 0)
    def _():
        run_ref[...] = jnp.zeros_like(run_ref)

    scores = 1.0 / (1.0 + jnp.exp(-lg_ref[...]))
    sel = scores + eb_ref[...]
    neg = -jnp.inf
    g_iota = lax.broadcasted_iota(jnp.int32, (GROUP_SIZE, tn), 0)
    gs = []
    for g in range(N_GROUPS):
        blk = sel[g * GROUP_SIZE:(g + 1) * GROUP_SIZE, :]
        m1 = jnp.max(blk, axis=0, keepdims=True)
        i1 = jnp.min(jnp.where(blk == m1, g_iota, GROUP_SIZE), axis=0, keepdims=True)
        m2 = jnp.max(jnp.where(g_iota == i1, neg, blk), axis=0, keepdims=True)
        gs.append(m1 + m2)
    masked = []
    for g in range(N_GROUPS):
        rank = jnp.zeros((1, tn), jnp.float32)
        for gp in range(N_GROUPS):
            if gp == g:
                continue
            beats = (gs[gp] > gs[g]) | ((gs[gp] == gs[g]) & (gp < g))
            rank = rank + jnp.where(beats, 1.0, 0.0)
        keep = rank < TOPK_GROUPS
        masked.append(jnp.where(keep, sel[g * GROUP_SIZE:(g + 1) * GROUP_SIZE, :], neg))
    masked = jnp.concatenate(masked, axis=0)
    e_iota = lax.broadcasted_iota(jnp.int32, (N_EXPERTS, tn), 0)
    idxs, ws, hits = [], [], []
    for _ in range(TOP_K):
        m = jnp.max(masked, axis=0, keepdims=True)
        idx = jnp.min(jnp.where(masked == m, e_iota, N_EXPERTS), axis=0, keepdims=True)
        hit = e_iota == idx
        ws.append(jnp.sum(jnp.where(hit, scores, 0.0), axis=0, keepdims=True))
        masked = jnp.where(hit, neg, masked)
        idxs.append(idx)
        hits.append(hit)
    wsum = ws[0]
    for k in range(1, TOP_K):
        wsum = wsum + ws[k]
    eidx_ref[...] = jnp.concatenate(idxs, axis=0)
    w_ref[...] = jnp.concatenate(ws, axis=0) / wsum * ROUTED_SCALE
    member = jnp.zeros((N_EXPERTS, tn), jnp.float32)
    for hit in hits:
        member = member + jnp.where(hit, 1.0, 0.0)
    total = run_ref[...] + jnp.sum(member, axis=1, keepdims=True)
    run_ref[...] = total
    cnt_ref[...] = total.astype(jnp.int32)


def _route_call(lg_t, e_bias):
    n_tok = lg_t.shape[1]
    tn = ROUTE_TN
    tok_blk = pl.BlockSpec((TOP_K, tn), lambda i: (0, i))
    return pl.pallas_call(
        _route_kernel,
        grid=(n_tok // tn,),
        in_specs=[pl.BlockSpec((N_EXPERTS, tn), lambda i: (0, i)),
                  pl.BlockSpec((N_EXPERTS, 1), lambda i: (0, 0))],
        out_specs=[tok_blk, tok_blk, pl.BlockSpec((N_EXPERTS, 1), lambda i: (0, 0))],
        out_shape=[jax.ShapeDtypeStruct((TOP_K, n_tok), jnp.int32),
                   jax.ShapeDtypeStruct((TOP_K, n_tok), jnp.float32),
                   jax.ShapeDtypeStruct((N_EXPERTS, 1), jnp.int32)],
        scratch_shapes=[pltpu.VMEM((N_EXPERTS, 1), jnp.float32)],
        compiler_params=pltpu.CompilerParams(vmem_limit_bytes=VMEM_LIMIT,
                                             dimension_semantics=("arbitrary",)),
        name="route",
    )(lg_t, e_bias.reshape(N_EXPERTS, 1))


def _moe_weight_copies(w_hbm, wf, sem, e, slot):
    return [pltpu.make_async_copy(w.at[e], f.at[slot], sem.at[slot]) for w, f in zip(w_hbm, wf)]


def _moe_row_copy(ybuf, yk_hbm, osem, slot, r, dst_row):
    return pltpu.make_async_copy(ybuf.at[slot, pl.ds(r * PACK_CHUNKS, PACK_CHUNKS), :],
                                 yk_hbm.at[pl.ds(pl.multiple_of(dst_row, PACK_CHUNKS), PACK_CHUNKS), :],
                                 osem.at[slot])


def _moe_kernel(blk_e_ref, first_ref, wslot_ref, next1_e_ref, next2_e_ref, nvalid_ref, tok0_ref, tokn_ref,
                dstp_ref, h2p_ref, w1_hbm, w3_hbm, w2_hbm, yk_hbm, xt, ybuf, wf1, wf3, wf2, wb1, wb3, wb2,
                sem, osem):
    i = pl.program_id(0)
    nvalid = nvalid_ref[0]

    stride = MOE_XT_STRIDE
    w_hbm = (w1_hbm, w3_hbm, w2_hbm)
    wf = (wf1, wf3, wf2)
    xslot = i % 2
    oslot = i % MOE_OUT_SLOTS
    pslot = (i + MOE_OUT_SLOTS - 1) % MOE_OUT_SLOTS

    def gather_rows(tok_ref, slot):
        for r in range(MOE_TM):
            src = pl.multiple_of(tok_ref[0, 0, r], PACK_CHUNKS)
            xt[slot, pl.ds(r, PACK_CHUNKS, stride=stride), :] = h2p_ref[pl.ds(src, PACK_CHUNKS), :]

    def send_prev_rows():
        for r in range(MOE_TM):
            _moe_row_copy(ybuf, yk_hbm, osem, pslot, r, dstp_ref[0, 0, r]).start(priority=r % 2)

    @pl.when(i == 0)
    def _():
        ybuf[...] = jnp.zeros_like(ybuf)
        blk_rows = MOE_TM * PACK_CHUNKS
        for s in range(MOE_OUT_SLOTS):
            cp = pltpu.make_async_copy(
                ybuf.at[s], yk_hbm.at[pl.ds(yk_hbm.shape[0] - (MOE_OUT_SLOTS - s) * blk_rows, blk_rows), :],
                osem.at[s])
            cp.start()
            cp.wait()
        gather_rows(tok0_ref, 0)
        for cp in _moe_weight_copies(w_hbm, wf, sem, blk_e_ref[0], 0):
            cp.start()

        @pl.when(next1_e_ref[0] >= 0)
        def _():
            for cp in _moe_weight_copies(w_hbm, wf, sem, next1_e_ref[0], 1):
                cp.start()

    @pl.when((i >= 2) & (i <= nvalid + 2))
    def _():
        for r in range(MOE_TM):
            _moe_row_copy(ybuf, yk_hbm, osem, oslot, r, 0).wait()

    @pl.when(i == nvalid)
    def _():
        send_prev_rows()

    @pl.when(i < nvalid)
    def _():
        @pl.when(first_ref[i] == 1)
        def _():
            slot = wslot_ref[i]
            for cp in _moe_weight_copies(w_hbm, wf, sem, blk_e_ref[i], slot):
                cp.wait()

            @pl.when(next2_e_ref[i] >= 0)
            def _():
                ahead = (slot + MOE_W_SLOTS - 1) % MOE_W_SLOTS
                for cp in _moe_weight_copies(w_hbm, wf, sem, next2_e_ref[i], ahead):
                    cp.start()

            wb1[...] = wf1[slot].astype(jnp.bfloat16)
            wb3[...] = wf3[slot].astype(jnp.bfloat16)
            wb2[...] = wf2[slot].astype(jnp.bfloat16)

        lo, hi = [], []
        for j in range(PACK_CHUNKS):
            word = xt[xslot, pl.ds(j * stride, MOE_TM), :]
            lo.append(lax.bitcast_convert_type(lax.shift_left(word, 16), jnp.float32).astype(jnp.bfloat16))
            hi.append(lax.bitcast_convert_type(word & _HI_MASK, jnp.float32).astype(jnp.bfloat16))
        x = jnp.concatenate(lo + hi, axis=-1)
        a = jnp.dot(x, wb1[...], preferred_element_type=jnp.float32)
        gather_rows(tokn_ref, 1 - xslot)
        b = jnp.dot(x, wb3[...], preferred_element_type=jnp.float32)
        h = (_silu(a) * b).astype(jnp.bfloat16)
        send_prev_rows()
        y = jnp.dot(h, wb2[...], preferred_element_type=jnp.float32)
        bits = lax.bitcast_convert_type(y.astype(jnp.bfloat16).astype(jnp.float32), jnp.int32)
        word = (bits[:, HALF:] & _HI_MASK) | lax.shift_right_logical(bits[:, :HALF], 16)
        for j in range(PACK_CHUNKS):
            ybuf[oslot, pl.ds(j, MOE_TM, stride=PACK_CHUNKS), :] = word[:, j * LANES:(j + 1) * LANES]


def _moe_call(blk_e, first, wslot, next1_e, next2_e, nvalid, buf_tok3, dst3, h2p, w1, w3, w2, n_rows):
    nblk = buf_tok3.shape[0]
    D = D_MODEL
    hbm = pl.BlockSpec(memory_space=pl.ANY)

    def blk_ids(shift, n):
        return pl.BlockSpec((1, 1, MOE_TM), lambda i, *_: (jnp.minimum(i + shift, n - 1), 0, 0),
                            memory_space=pltpu.SMEM)

    grid_spec = pltpu.PrefetchScalarGridSpec(
        num_scalar_prefetch=6,
        grid=(nblk + MOE_OUT_SLOTS - 1,),
        in_specs=[blk_ids(0, nblk), blk_ids(1, nblk), blk_ids(0, nblk + 1),
                  pl.BlockSpec(h2p.shape, lambda i, *_: (0, 0), pipeline_mode=pl.Buffered(1)),
                  hbm, hbm, hbm],
        out_specs=hbm,
        scratch_shapes=[pltpu.VMEM((2, PACK_CHUNKS * MOE_XT_STRIDE, LANES), jnp.int32),
                        pltpu.VMEM((MOE_OUT_SLOTS, MOE_TM * PACK_CHUNKS, LANES), jnp.int32),
                        pltpu.VMEM((MOE_W_SLOTS, D, EXPERT_DIM), jnp.float32),
                        pltpu.VMEM((MOE_W_SLOTS, D, EXPERT_DIM), jnp.float32),
                        pltpu.VMEM((MOE_W_SLOTS, EXPERT_DIM, D), jnp.float32),
                        pltpu.VMEM((D, EXPERT_DIM), jnp.bfloat16),
                        pltpu.VMEM((D, EXPERT_DIM), jnp.bfloat16),
                        pltpu.VMEM((EXPERT_DIM, D), jnp.bfloat16),
                        pltpu.SemaphoreType.DMA((MOE_W_SLOTS,)),
                        pltpu.SemaphoreType.DMA((MOE_OUT_SLOTS,))],
    )
    return pl.pallas_call(
        _moe_kernel,
        grid_spec=grid_spec,
        out_shape=jax.ShapeDtypeStruct((n_rows * PACK_CHUNKS, LANES), jnp.int32),
        compiler_params=pltpu.CompilerParams(vmem_limit_bytes=MOE_VMEM_LIMIT,
                                             dimension_semantics=("arbitrary",)),
        name="moe",
    )(blk_e, first, wslot, next1_e, next2_e, nvalid, buf_tok3, buf_tok3, dst3, h2p, w1, w3, w2)


def _combine_kernel(yk_ref, wk_ref, x1_ref, h2_ref, g2_ref, ws1_ref, ws3_ref, ws2_ref, fg_ref, o_ref):
    tm = COMBINE_TM
    h2 = h2_ref[...]
    a = jnp.dot(h2, ws1_ref[...], preferred_element_type=jnp.float32)
    b = jnp.dot(h2, ws3_ref[...], preferred_element_type=jnp.float32)
    acc = jnp.dot((_silu(a) * b).astype(jnp.bfloat16), ws2_ref[...], preferred_element_type=jnp.float32)
    wk = wk_ref[...]
    for k in range(TOP_K):
        lo, hi = [], []
        for j in range(PACK_CHUNKS):
            word = yk_ref[pl.ds(k * tm * PACK_CHUNKS + j, tm, stride=PACK_CHUNKS), :]
            lo.append(lax.bitcast_convert_type(lax.shift_left(word, 16), jnp.float32))
            hi.append(lax.bitcast_convert_type(word & _HI_MASK, jnp.float32))
        acc = acc + wk[:, k:k + 1] * jnp.concatenate(lo + hi, axis=-1)
    x = x1_ref[...] + g2_ref[0] * acc
    o_ref[...] = x * lax.rsqrt(jnp.mean(x * x, axis=-1, keepdims=True) + EPS) * fg_ref[...]


def _combine_call(yk, wk, x1, h2b, g2, ws1, ws3, ws2, fg, seq):
    n_tok, D = x1.shape
    tm = COMBINE_TM
    per_seq = seq // tm
    row = lambda i: (i, 0)
    const = lambda i: (0, 0)
    n_steps = n_tok // tm
    return pl.pallas_call(
        _combine_kernel,
        grid=(n_steps,),
        in_specs=[pl.BlockSpec((TOP_K * tm * PACK_CHUNKS, LANES), row),
                  pl.BlockSpec((tm, TOP_K), row),
                  pl.BlockSpec((tm, D), row),
                  pl.BlockSpec((tm, D), row),
                  pl.BlockSpec((1, 1, D), lambda i: (i // per_seq, 0, 0)),
                  pl.BlockSpec((D, SHARED_DIM), const),
                  pl.BlockSpec((D, SHARED_DIM), const),
                  pl.BlockSpec((SHARED_DIM, D), const),
                  pl.BlockSpec((1, D), const)],
        out_specs=pl.BlockSpec((tm, D), row),
        out_shape=jax.ShapeDtypeStruct((n_tok, D), jnp.float32),
        compiler_params=pltpu.CompilerParams(vmem_limit_bytes=VMEM_LIMIT),
        name="combine",
    )(yk, wk, x1, h2b, g2, ws1, ws3, ws2, fg)


def _pair_heads(a, axis):
    shape = a.shape
    a = a.reshape(shape[:axis] + (SWA_KV_HEADS, SWA_GROUP, HEAD_DIM) + shape[axis + 1:])
    a = jnp.swapaxes(a, axis, axis + 1)
    return a.reshape(shape)


def _bias_vectors(rel_table, seq):
    vec = rel_table.astype(jnp.float32)[_t5_bucket(jnp.arange(seq, dtype=jnp.int32))]
    m = jnp.arange(2 * SWA_BLOCK)
    d_a = jnp.clip(((-m) % (2 * SWA_BLOCK)) - SWA_BLOCK, 0, seq - 1)
    tab_sw = vec[d_a][:, :SWA_Q_HEADS].T
    nb = seq // MOBA_BLOCK
    d_b = jnp.clip(jnp.arange(nb)[:, None] * MOBA_BLOCK - MOBA_BLOCK + jnp.arange(2 * MOBA_BLOCK)[None, :],
                   0, seq - 1)
    tab_mb = jnp.transpose(vec[d_b][..., SWA_Q_HEADS:], (2, 0, 1))
    return tab_sw, tab_mb


def _route_plan(eidx_t, counts, n_tok):
    A = n_tok * TOP_K
    P = A + N_EXPERTS * MOE_TM
    nblk = P // MOE_TM
    counts = counts.reshape(N_EXPERTS)
    pcounts = (counts + MOE_TM - 1) // MOE_TM * MOE_TM
    pends = jnp.cumsum(pcounts)
    pstarts = pends - pcounts
    blk_start = jnp.arange(nblk, dtype=jnp.int32) * MOE_TM
    blk_e = jnp.minimum(jnp.sum((pends[None, :] <= blk_start[:, None]).astype(jnp.int32), axis=1),
                        N_EXPERTS - 1).astype(jnp.int32)
    nvalid = (pends[-1] // MOE_TM).astype(jnp.int32).reshape(1)
    first = jnp.concatenate([jnp.ones((1,), jnp.int32), (blk_e[1:] != blk_e[:-1]).astype(jnp.int32)])
    wslot = ((jnp.cumsum(first) - 1) % MOE_W_SLOTS).astype(jnp.int32)
    run_end = pends[blk_e] // MOE_TM

    def expert_at(blk):
        return jnp.where(blk < nvalid[0], blk_e[jnp.minimum(blk, nblk - 1)], -1).astype(jnp.int32)

    next1_e = expert_at(run_end)
    next2_e = jnp.where(next1_e >= 0, expert_at(run_end[jnp.minimum(run_end, nblk - 1)]), -1)
    slot = (jnp.arange(n_tok, dtype=jnp.int32) * TOP_K)[None, :] + jnp.arange(TOP_K, dtype=jnp.int32)[:, None]
    real_keys = (eidx_t << KEY_SHIFT) | slot
    s = jnp.arange(MOE_TM, dtype=jnp.int32)[None, :]
    e = jnp.arange(N_EXPERTS, dtype=jnp.int32)[:, None]
    pad_keys = jnp.where(s < (pcounts - counts)[:, None], (e << KEY_SHIFT) | (A + s),
                         (N_EXPERTS << KEY_SHIFT) | (A + s))
    keys = jnp.sort(jnp.concatenate([real_keys.reshape(A), pad_keys.reshape(N_EXPERTS * MOE_TM)]))
    slot_sorted = keys & ((1 << KEY_SHIFT) - 1)
    real = slot_sorted < A
    tok = slot_sorted // TOP_K
    buf_tok = jnp.where(real, tok, 0).astype(jnp.int32)
    pick = slot_sorted % TOP_K
    dst = ((tok // COMBINE_TM) * TOP_K + pick) * COMBINE_TM + tok % COMBINE_TM
    row = jnp.arange(P, dtype=jnp.int32)
    spare = A + ((row // MOE_TM) % 2) * MOE_TM + row % MOE_TM
    dst = jnp.where(real, dst, spare).astype(jnp.int32)
    dummy = A + 2 * MOE_TM + jnp.arange(MOE_TM, dtype=jnp.int32)
    dst = jnp.concatenate([dummy, dst]) * PACK_CHUNKS
    blocks = (blk_e, first, wslot, next1_e, next2_e, nvalid)
    return ((buf_tok * PACK_CHUNKS).reshape(nblk, 1, MOE_TM), dst.reshape(nblk + 1, 1, MOE_TM), blocks)


def kernel(x, c, w_ada, b_ada, norm1_g, w_in, sinks, rel_table, out_norm_a, out_norm_b, w_out, norm2_g,
           w_router, e_bias, w1, w3, w2, ws1, ws3, ws2, final_g):
    B, S, D = x.shape
    assert w_ada.shape[0] == 1, "single-layer stack only"
    assert D == D_MODEL and S % INPROJ_TM == 0 and S % MOBA_BLOCK == 0
    assert B * S * TOP_K + MOE_TM <= (1 << KEY_SHIFT)
    n_tok = B * S
    bf16 = jnp.bfloat16
    drop = lambda a: a.reshape(a.shape[1:])
    tab_sw, tab_mb = _bias_vectors(rel_table, S)
    x2 = x.reshape(n_tok, D)

    mod = _ada_call(c, drop(w_ada), drop(b_ada))
    sh1, sc1, g1, sh2, sc2, g2 = [m.reshape(B, 1, D) for m in jnp.split(mod, 6, axis=-1)]
    w_in2 = drop(w_in)
    k_cols = jnp.concatenate([jnp.ones((SWA_WIDTH,)), jnp.full((SWA_KV_WIDTH,), ATTN_SCALE),
                              jnp.ones((SWA_KV_WIDTH + MOBA_WIDTH,)), jnp.full((MOBA_WIDTH,), ATTN_SCALE),
                              jnp.ones((MOBA_WIDTH,))]).astype(jnp.float32)
    w_in_p = (jnp.concatenate([_pair_heads(w_in2[:, :SWA_WIDTH], 1), w_in2[:, SWA_WIDTH:]], axis=1)
              * k_cols[None, :]).astype(bf16)
    qa, ka, va, qb, kb, vb = _inproj_call(x2, norm1_g.reshape(1, D), sh1, sc1, w_in_p, S)
    sink_col = jnp.repeat(sinks.reshape(SWA_Q_HEADS).astype(jnp.float32), SWA_BLOCK).reshape(
        SWA_KV_HEADS, SWA_GROUP * SWA_BLOCK, 1)
    ya = _swa_call(qa.reshape(B, S, SWA_WIDTH), ka.reshape(B, S, SWA_KV_WIDTH),
                   va.reshape(B, S, SWA_KV_WIDTH), tab_sw, sink_col)
    yb = _moba_call(qb.reshape(B, S, MOBA_WIDTH), kb.reshape(B, S, MOBA_WIDTH),
                    vb.reshape(B, S, MOBA_WIDTH), tab_mb)
    w_rt = drop(w_router).T
    wr_hi = w_rt.astype(bf16)
    wr_lo = (w_rt - wr_hi.astype(jnp.float32)).astype(bf16)
    w_out2 = drop(w_out)
    w_out_p = jnp.concatenate([_pair_heads(w_out2[:SWA_WIDTH], 0), w_out2[SWA_WIDTH:]], axis=0).astype(bf16)
    x1, h2p, h2b, lg_t = _outproj_call(
        ya.reshape(n_tok, SWA_WIDTH), yb.reshape(n_tok, MOBA_WIDTH), x2,
        _pair_heads(out_norm_a.reshape(SWA_WIDTH), 0).reshape(1, SWA_WIDTH), out_norm_b.reshape(1, MOBA_WIDTH),
        w_out_p, g1, norm2_g.reshape(1, D), sh2, sc2, wr_hi, wr_lo, S)
    eidx_t, w_t, counts = _route_call(lg_t, e_bias.reshape(N_EXPERTS))
    buf_tok3, dst3, blocks = _route_plan(eidx_t, counts, n_tok)
    yk = _moe_call(*blocks, buf_tok3, dst3, h2p, drop(w1), drop(w3), drop(w2),
                   n_tok * TOP_K + MOE_OUT_SLOTS * MOE_TM)
    out = _combine_call(yk, w_t.T, x1, h2b, g2, drop(ws1).astype(bf16), drop(ws3).astype(bf16),
                        drop(ws2).astype(bf16), final_g.reshape(1, D), S)
    return out.reshape(B, S, D)
```

```python
import math

import jax
import jax.numpy as jnp
from jax import lax
from jax.experimental import pallas as pl
from jax.experimental.pallas import tpu as pltpu

D_MODEL = 1024
HEAD_DIM = 64
SWA_Q_HEADS = 8
SWA_KV_HEADS = 2
SWA_GROUP = SWA_Q_HEADS // SWA_KV_HEADS
SWA_WINDOW = 128
SWA_BLOCK = 128
SWA_WIDTH = SWA_Q_HEADS * HEAD_DIM
SWA_KV_WIDTH = SWA_KV_HEADS * HEAD_DIM
MOBA_HEADS = 8
MOBA_BLOCK = 256
MOBA_TOPK = 3
MOBA_WIDTH = MOBA_HEADS * HEAD_DIM
N_HEADS = SWA_Q_HEADS + MOBA_HEADS
MIX_WIDTH = SWA_WIDTH + MOBA_WIDTH
IN_COLS = SWA_WIDTH + 2 * SWA_KV_WIDTH + 3 * MOBA_WIDTH
ATTN_SCALE = HEAD_DIM ** -0.5
REL_BUCKETS = 32
REL_MAX_DIST = 1024
N_EXPERTS = 256
TOP_K = 8
N_GROUPS = 8
TOPK_GROUPS = 4
GROUP_SIZE = N_EXPERTS // N_GROUPS
EXPERT_DIM = 256
SHARED_DIM = 256
ROUTED_SCALE = 2.5
EPS = 1e-6
NEG_INF = -1e30

LANES = 128
SUBLANES = 8
ROW_CHUNKS = D_MODEL // LANES
HALF = D_MODEL // 2
PACK_CHUNKS = HALF // LANES

ADA_TN = 512
INPROJ_TM = 512
OUTPROJ_TM = 256
ROUTE_TN = 512
MOE_TM = 256
MOE_XT_STRIDE = MOE_TM + SUBLANES
MOE_OUT_SLOTS = 3
MOE_W_SLOTS = 3
MOE_ROW_PRIORITY = 0
MOE_WEIGHT_PRIORITY = 1
COMBINE_TM = 128
KEY_SHIFT = 18
VMEM_LIMIT = 48 * 1024 * 1024
MOE_VMEM_LIMIT = 58 * 1024 * 1024

_HIGHEST = lax.Precision.HIGHEST
_NT = (((1,), (1,)), ((), ()))
_TN = (((0,), (0,)), ((), ()))
_HI_MASK = -65536


def _silu(a):
    return a * (1.0 / (1.0 + jnp.exp(-a)))


def _t5_bucket(dist):
    n = jnp.maximum(dist, 0)
    max_exact = REL_BUCKETS // 2
    nf = jnp.maximum(n, 1).astype(jnp.float32)
    large = max_exact + (jnp.log(nf / max_exact) / math.log(REL_MAX_DIST / max_exact)
                         * (REL_BUCKETS - max_exact)).astype(jnp.int32)
    large = jnp.minimum(large, REL_BUCKETS - 1)
    return jnp.where(n < max_exact, n, large)


def _ada_kernel(c_ref, w_ref, b_ref, o_ref):
    c = c_ref[...]
    o_ref[...] = jnp.dot(_silu(c), w_ref[...], precision=_HIGHEST,
                         preferred_element_type=jnp.float32) + b_ref[...]


def _ada_call(c, w_ada, b_ada):
    B, D = c.shape
    n_out = w_ada.shape[1]
    return pl.pallas_call(
        _ada_kernel,
        grid=(n_out // ADA_TN,),
        in_specs=[pl.BlockSpec((B, D), lambda j: (0, 0)),
                  pl.BlockSpec((D, ADA_TN), lambda j: (0, j)),
                  pl.BlockSpec((1, ADA_TN), lambda j: (0, j))],
        out_specs=pl.BlockSpec((B, ADA_TN), lambda j: (0, j)),
        out_shape=jax.ShapeDtypeStruct((B, n_out), jnp.float32),
        name="adaln",
    )(c, w_ada, b_ada.reshape(1, n_out))


def _inproj_kernel(x_ref, g_ref, sh_ref, sc_ref, w_ref, qa_ref, ka_ref, va_ref, qb_ref, kb_ref, vb_ref):
    x = x_ref[...]
    h = x * lax.rsqrt(jnp.mean(x * x, axis=-1, keepdims=True) + EPS) * g_ref[...]
    h = h * (1.0 + sc_ref[0]) + sh_ref[0]
    p = jnp.dot(h.astype(jnp.bfloat16), w_ref[...], preferred_element_type=jnp.float32)
    off = 0
    for ref in (qa_ref, ka_ref, va_ref, qb_ref, kb_ref, vb_ref):
        width = ref.shape[-1]
        ref[...] = p[:, off:off + width].astype(ref.dtype)
        off += width


def _inproj_call(x2, g, sh, sc, w_bf16, seq):
    n_tok, D = x2.shape
    tm = INPROJ_TM
    per_seq = seq // tm
    widths = (SWA_WIDTH, SWA_KV_WIDTH, SWA_KV_WIDTH, MOBA_WIDTH, MOBA_WIDTH, MOBA_WIDTH)
    mod_spec = pl.BlockSpec((1, 1, D), lambda i: (i // per_seq, 0, 0))
    return pl.pallas_call(
        _inproj_kernel,
        grid=(n_tok // tm,),
        in_specs=[pl.BlockSpec((tm, D), lambda i: (i, 0)),
                  pl.BlockSpec((1, D), lambda i: (0, 0)),
                  mod_spec, mod_spec,
                  pl.BlockSpec((D, IN_COLS), lambda i: (0, 0))],
        out_specs=[pl.BlockSpec((tm, w), lambda i: (i, 0)) for w in widths],
        out_shape=[jax.ShapeDtypeStruct((n_tok, w), jnp.bfloat16) for w in widths],
        compiler_params=pltpu.CompilerParams(vmem_limit_bytes=VMEM_LIMIT),
        name="inproj",
    )(x2, g, sh, sc, w_bf16)


def _swa_kernel(q_ref, kc_ref, kp_ref, vc_ref, vp_ref, tab_ref, sink_ref, o_ref, bias_ref):
    n = pl.program_id(1)
    rows = SWA_GROUP * SWA_BLOCK
    lo = lax.broadcasted_iota(jnp.int32, (1, LANES), 1) < HEAD_DIM

    @pl.when((pl.program_id(0) == 0) & (n == 0))
    def _():
        for h in range(SWA_Q_HEADS):
            kvh, c = divmod(h, SWA_GROUP)
            row = jnp.broadcast_to(tab_ref[h:h + 1, :], (SWA_BLOCK, 2 * SWA_BLOCK))
            bias_ref[kvh, c * SWA_BLOCK:(c + 1) * SWA_BLOCK, :] = pltpu.roll(row, 0, 1, stride=1, stride_axis=0)

    kk = jnp.concatenate([kp_ref[0], kc_ref[0]], axis=0)
    vv = jnp.concatenate([vp_ref[0], vc_ref[0]], axis=0)
    qi = lax.broadcasted_iota(jnp.int32, (rows, 2 * SWA_BLOCK), 0) & (SWA_BLOCK - 1)
    kj = lax.broadcasted_iota(jnp.int32, (rows, 2 * SWA_BLOCK), 1)
    dist = qi + SWA_BLOCK - kj
    valid = (dist >= 0) & (dist < SWA_WINDOW) & ((kj >= SWA_BLOCK) | (n > 0))
    qs = jnp.concatenate([q_ref[0, :, c * LANES:(c + 1) * LANES] for c in range(SWA_GROUP)], axis=0)
    masks = (lo, jnp.logical_not(lo))
    scores = [lax.dot_general(jnp.where(hm, qs, jnp.zeros_like(qs)), kk, _NT, preferred_element_type=jnp.float32)
              for hm in masks]
    ps, sink_terms = [], []
    for kvh, s in enumerate(scores):
        s = s + bias_ref[kvh]
        s = jnp.where(valid, s, NEG_INF)
        sink = sink_ref[kvh]
        m = jnp.maximum(jnp.max(s, axis=-1, keepdims=True), sink)
        ps.append(jnp.exp(s - m).astype(jnp.bfloat16))
        sink_terms.append(jnp.exp(sink - m))
    accs = [jnp.dot(p, jnp.where(hm, vv, jnp.ones_like(vv)), preferred_element_type=jnp.float32)
            for hm, p in zip(masks, ps)]
    outs = [acc / (pltpu.roll(acc, HEAD_DIM, 1) + sink_term) for acc, sink_term in zip(accs, sink_terms)]
    for c in range(SWA_GROUP):
        blk = jnp.where(lo, outs[0][c * SWA_BLOCK:(c + 1) * SWA_BLOCK],
                        outs[1][c * SWA_BLOCK:(c + 1) * SWA_BLOCK])
        o_ref[0, :, c * LANES:(c + 1) * LANES] = blk.astype(o_ref.dtype)


def _swa_call(qa, ka, va, tab_sw, sink_col):
    B, S, _ = qa.shape
    nb = S // SWA_BLOCK
    rows = SWA_GROUP * SWA_BLOCK
    cur = lambda b, n: (b, n, 0)
    prev = lambda b, n: (b, jnp.maximum(n - 1, 0), 0)
    kv_blk = (1, SWA_BLOCK, SWA_KV_WIDTH)
    return pl.pallas_call(
        _swa_kernel,
        grid=(B, nb),
        in_specs=[pl.BlockSpec((1, SWA_BLOCK, SWA_WIDTH), cur),
                  pl.BlockSpec(kv_blk, cur), pl.BlockSpec(kv_blk, prev),
                  pl.BlockSpec(kv_blk, cur), pl.BlockSpec(kv_blk, prev),
                  pl.BlockSpec((SWA_Q_HEADS, 2 * SWA_BLOCK), lambda b, n: (0, 0)),
                  pl.BlockSpec((SWA_KV_HEADS, rows, 1), lambda b, n: (0, 0, 0))],
        out_specs=pl.BlockSpec((1, SWA_BLOCK, SWA_WIDTH), cur),
        out_shape=jax.ShapeDtypeStruct((B, S, SWA_WIDTH), jnp.bfloat16),
        scratch_shapes=[pltpu.VMEM((SWA_KV_HEADS, rows, 2 * SWA_BLOCK), jnp.float32)],
        compiler_params=pltpu.CompilerParams(vmem_limit_bytes=VMEM_LIMIT,
                                             dimension_semantics=("arbitrary", "arbitrary")),
        name="swa",
    )(qa, ka, ka, va, va, tab_sw, sink_col)


def _moba_kernel(q_ref, k_ref, v_ref, tab_ref, o_ref, bias_ref, sel_ref, m_ref, acc_ref):
    S = q_ref.shape[1]
    nb = S // MOBA_BLOCK
    blk = MOBA_BLOCK
    lane_lo = lax.broadcasted_iota(jnp.int32, (1, LANES), 1) < HEAD_DIM
    head_masks = (lane_lo, jnp.logical_not(lane_lo))

    @pl.when(pl.program_id(1) == 0)
    def _():
        for hh in range(2):
            for d in range(nb):
                row = jnp.broadcast_to(tab_ref[hh, d:d + 1, :], (blk, 2 * blk))
                bias_ref[hh, d] = pltpu.roll(row, blk, 1, stride=1, stride_axis=0)[:, :blk]

    kmeans = [jnp.mean(k_ref[0, j * blk:(j + 1) * blk, :].astype(jnp.float32), axis=0, keepdims=True)
              for j in range(nb)]
    kmean = jnp.concatenate(kmeans, axis=0)
    causal_t = (lax.broadcasted_iota(jnp.int32, (blk, blk), 0)
                <= lax.broadcasted_iota(jnp.int32, (blk, blk), 1))
    blk_id = lax.broadcasted_iota(jnp.int32, (nb, blk), 0)

    def rows(ref, b):
        return ref[0, pl.ds(pl.multiple_of(b * blk, blk), blk), :]

    def masked_q(i, hh):
        q = rows(q_ref, i)
        return jnp.where(head_masks[hh], q, jnp.zeros_like(q))

    def masked_v(j, hh):
        v = rows(v_ref, j)
        return jnp.where(head_masks[hh], v, jnp.ones_like(v))

    def own_blocks(q_blocks):
        chains = [(i, hh) for i in q_blocks for hh in range(2)]
        qms = [masked_q(i, hh) for i, hh in chains]
        gates = [lax.dot_general(kmean, qm.astype(jnp.float32), _NT, precision=_HIGHEST,
                                 preferred_element_type=jnp.float32) for qm in qms]
        scores = [lax.dot_general(rows(k_ref, i), qm, _NT, preferred_element_type=jnp.float32)
                  for (i, hh), qm in zip(chains, qms)]
        sels = []
        for (i, hh), gate in zip(chains, gates):
            rank = jnp.zeros((nb, blk), jnp.float32)
            for jp in range(nb):
                gj = gate[jp:jp + 1, :]
                beats = jnp.where(gj > gate, 1.0, jnp.where((gj == gate) & (jp < blk_id), 1.0, 0.0))
                rank = rank + jnp.where(jp < i, beats, 0.0)
            sels.append(jnp.where(rank < MOBA_TOPK, 1.0, 0.0))
        ms, ps = [], []
        for (i, hh), s in zip(chains, scores):
            s = s + bias_ref[hh, 0]
            s = jnp.where(causal_t, s, NEG_INF)
            m = jnp.max(s, axis=0, keepdims=True)
            ps.append(jnp.exp(s - m).astype(jnp.bfloat16))
            ms.append(m)
        accs = [lax.dot_general(masked_v(i, hh), p, _TN, preferred_element_type=jnp.float32)
                for (i, hh), p in zip(chains, ps)]
        for (i, hh), sel, m, acc in zip(chains, sels, ms, accs):
            sel_ref[2 * i + hh] = sel
            m_ref[2 * i + hh] = jnp.broadcast_to(m, (SUBLANES, blk))
            acc_ref[2 * i + hh] = acc

    def past_blocks(d, q_blocks):
        chains = [(i, hh) for i in q_blocks for hh in range(2)]
        scores = [lax.dot_general(rows(k_ref, i - d), masked_q(i, hh), _NT, preferred_element_type=jnp.float32)
                  for i, hh in chains]
        ms, ps, alphas = [], [], []
        for (i, hh), s in zip(chains, scores):
            s = s + bias_ref[hh, d]
            picked = jnp.sum(jnp.where(blk_id == i - d, sel_ref[2 * i + hh], 0.0), axis=0, keepdims=True) > 0.5
            s = jnp.where(picked, s, NEG_INF)
            m = m_ref[2 * i + hh][0:1]
            m_new = jnp.maximum(m, jnp.max(s, axis=0, keepdims=True))
            ps.append(jnp.exp(s - m_new).astype(jnp.bfloat16))
            alphas.append(jnp.exp(m - m_new))
            ms.append(m_new)
        pvs = [lax.dot_general(masked_v(i - d, hh), p, _TN, preferred_element_type=jnp.float32)
               for (i, hh), p in zip(chains, ps)]
        accs = [alpha * acc_ref[2 * i + hh] + pv for (i, hh), alpha, pv in zip(chains, alphas, pvs)]
        for (i, hh), m_new, acc in zip(chains, ms, accs):
            m_ref[2 * i + hh] = jnp.broadcast_to(m_new, (SUBLANES, blk))
            acc_ref[2 * i + hh] = acc

    def own_pair(t, carry):
        own_blocks((2 * t, 2 * t + 1))
        return carry

    lax.fori_loop(0, nb // 2, own_pair, 0)

    def diagonal(d, carry):
        def pair(t, c):
            past_blocks(d, (d + 2 * t, d + 2 * t + 1))
            return c

        lax.fori_loop(0, (nb - d) // 2, pair, 0)

        @pl.when((nb - d) % 2 == 1)
        def _():
            past_blocks(d, (nb - 1,))

        return carry

    lax.fori_loop(1, nb, diagonal, 0)

    def finish(i, carry):
        a0 = acc_ref[2 * i]
        a1 = acc_ref[2 * i + 1]
        out_t = jnp.concatenate([a0[:HEAD_DIM] / a0[HEAD_DIM:], a1[HEAD_DIM:] / a1[:HEAD_DIM]], axis=0)
        o_ref[0, pl.ds(pl.multiple_of(i * blk, blk), blk), :] = out_t.T.astype(o_ref.dtype)
        return carry

    lax.fori_loop(0, nb, finish, 0)


def _moba_call(qb, kb, vb, tab_mb):
    B, S, _ = qb.shape
    nb = S // MOBA_BLOCK
    pairs = MOBA_HEADS // 2
    slab = pl.BlockSpec((1, S, LANES), lambda hp, b: (b, 0, hp))
    return pl.pallas_call(
        _moba_kernel,
        grid=(pairs, B),
        in_specs=[slab, slab, slab,
                  pl.BlockSpec((2, nb, 2 * MOBA_BLOCK), lambda hp, b: (hp, 0, 0))],
        out_specs=slab,
        out_shape=jax.ShapeDtypeStruct((B, S, MOBA_WIDTH), jnp.bfloat16),
        scratch_shapes=[pltpu.VMEM((2, nb, MOBA_BLOCK, MOBA_BLOCK), jnp.float32),
                        pltpu.VMEM((2 * nb, nb, MOBA_BLOCK), jnp.float32),
                        pltpu.VMEM((2 * nb, SUBLANES, MOBA_BLOCK), jnp.float32),
                        pltpu.VMEM((2 * nb, LANES, MOBA_BLOCK), jnp.float32)],
        compiler_params=pltpu.CompilerParams(vmem_limit_bytes=VMEM_LIMIT,
                                             dimension_semantics=("arbitrary", "arbitrary")),
        name="moba",
    )(qb, kb, vb, tab_mb)


def _outproj_kernel(ya_ref, yb_ref, x_ref, ga_ref, gb_ref, w_ref, g1_ref, n2_ref, sh_ref, sc_ref, wr_hi_ref, wr_lo_ref,
                    x1_ref, h2p_ref, h2b_ref, lg_ref):
    tm = x_ref.shape[0]

    def norm(y, g):
        y = y.astype(jnp.float32)
        return y * lax.rsqrt(jnp.mean(y * y, axis=-1, keepdims=True) + EPS) * g

    yn = jnp.concatenate([norm(ya_ref[...], ga_ref[...]), norm(yb_ref[...], gb_ref[...])], axis=-1)
    y = jnp.dot(yn.astype(jnp.bfloat16), w_ref[...], preferred_element_type=jnp.float32)
    x1 = x_ref[...] + g1_ref[0] * y
    x1_ref[...] = x1
    h2 = x1 * lax.rsqrt(jnp.mean(x1 * x1, axis=-1, keepdims=True) + EPS) * n2_ref[...]
    h2 = h2 * (1.0 + sc_ref[0]) + sh_ref[0]
    h2b = h2.astype(jnp.bfloat16)
    h2b_ref[...] = h2b
    bits = lax.bitcast_convert_type(h2b.astype(jnp.float32), jnp.int32)
    word = (bits[:, HALF:] & _HI_MASK) | lax.shift_right_logical(bits[:, :HALF], 16)
    for j in range(PACK_CHUNKS):
        h2p_ref[pl.ds(j, tm, stride=PACK_CHUNKS), :] = word[:, j * LANES:(j + 1) * LANES]
    h2_lo = (h2 - h2b.astype(jnp.float32)).astype(jnp.bfloat16)
    wr_hi = wr_hi_ref[...]
    lg = lax.dot_general(wr_hi, h2b, _NT, preferred_element_type=jnp.float32)
    lg = lg + lax.dot_general(wr_hi, h2_lo, _NT, preferred_element_type=jnp.float32)
    lg_ref[...] = lg + lax.dot_general(wr_lo_ref[...], h2b, _NT, preferred_element_type=jnp.float32)


def _outproj_call(ya, yb, x2, ga, gb, w_bf16, g1, n2, sh2, sc2, wr_hi, wr_lo, seq):
    n_tok, D = x2.shape
    tm = OUTPROJ_TM
    per_seq = seq // tm
    row = lambda i: (i, 0)
    const = lambda i: (0, 0)
    mod_spec = pl.BlockSpec((1, 1, D), lambda i: (i // per_seq, 0, 0))
    return pl.pallas_call(
        _outproj_kernel,
        grid=(n_tok // tm,),
        in_specs=[pl.BlockSpec((tm, SWA_WIDTH), row), pl.BlockSpec((tm, MOBA_WIDTH), row),
                  pl.BlockSpec((tm, D), row),
                  pl.BlockSpec((1, SWA_WIDTH), const), pl.BlockSpec((1, MOBA_WIDTH), const),
                  pl.BlockSpec((MIX_WIDTH, D), const),
                  mod_spec, pl.BlockSpec((1, D), const), mod_spec, mod_spec,
                  pl.BlockSpec((N_EXPERTS, D), const), pl.BlockSpec((N_EXPERTS, D), const)],
        out_specs=[pl.BlockSpec((tm, D), row),
                   pl.BlockSpec((tm * PACK_CHUNKS, LANES), row),
                   pl.BlockSpec((tm, D), row),
                   pl.BlockSpec((N_EXPERTS, tm), lambda i: (0, i))],
        out_shape=[jax.ShapeDtypeStruct((n_tok, D), jnp.float32),
                   jax.ShapeDtypeStruct((n_tok * PACK_CHUNKS, LANES), jnp.int32),
                   jax.ShapeDtypeStruct((n_tok, D), jnp.bfloat16),
                   jax.ShapeDtypeStruct((N_EXPERTS, n_tok), jnp.float32)],
        compiler_params=pltpu.CompilerParams(vmem_limit_bytes=VMEM_LIMIT),
        name="outproj",
    )(ya, yb, x2, ga, gb, w_bf16, g1, n2, sh2, sc2, wr_hi, wr_lo)


def _route_kernel(lg_ref, eb_ref, eidx_ref, w_ref, cnt_ref, run_ref):
    tn = lg_ref.shape[1]

    @pl.when(pl.program_id(0) == 0)
    def _():
        run_ref[...] = jnp.zeros_like(run_ref)

    scores = 1.0 / (1.0 + jnp.exp(-lg_ref[...]))
    sel = scores + eb_ref[...]
    neg = -jnp.inf
    g_iota = lax.broadcasted_iota(jnp.int32, (GROUP_SIZE, tn), 0)
    gs = []
    for g in range(N_GROUPS):
        blk = sel[g * GROUP_SIZE:(g + 1) * GROUP_SIZE, :]
        m1 = jnp.max(blk, axis=0, keepdims=True)
        i1 = jnp.min(jnp.where(blk == m1, g_iota, GROUP_SIZE), axis=0, keepdims=True)
        m2 = jnp.max(jnp.where(g_iota == i1, neg, blk), axis=0, keepdims=True)
        gs.append(m1 + m2)
    masked = []
    for g in range(N_GROUPS):
        rank = jnp.zeros((1, tn), jnp.float32)
        for gp in range(N_GROUPS):
            if gp == g:
                continue
            beats = (gs[gp] > gs[g]) | ((gs[gp] == gs[g]) & (gp < g))
            rank = rank + jnp.where(beats, 1.0, 0.0)
        keep = rank < TOPK_GROUPS
        masked.append(jnp.where(keep, sel[g * GROUP_SIZE:(g + 1) * GROUP_SIZE, :], neg))
    masked = jnp.concatenate(masked, axis=0)
    e_iota = lax.broadcasted_iota(jnp.int32, (N_EXPERTS, tn), 0)
    idxs, ws, hits = [], [], []
    for _ in range(TOP_K):
        m = jnp.max(masked, axis=0, keepdims=True)
        idx = jnp.min(jnp.where(masked == m, e_iota, N_EXPERTS), axis=0, keepdims=True)
        hit = e_iota == idx
        ws.append(jnp.sum(jnp.where(hit, scores, 0.0), axis=0, keepdims=True))
        masked = jnp.where(hit, neg, masked)
        idxs.append(idx)
        hits.append(hit)
    wsum = ws[0]
    for k in range(1, TOP_K):
        wsum = wsum + ws[k]
    eidx_ref[...] = jnp.concatenate(idxs, axis=0)
    w_ref[...] = jnp.concatenate(ws, axis=0) / wsum * ROUTED_SCALE
    member = jnp.zeros((N_EXPERTS, tn), jnp.float32)
    for hit in hits:
        member = member + jnp.where(hit, 1.0, 0.0)
    total = run_ref[...] + jnp.sum(member, axis=1, keepdims=True)
    run_ref[...] = total
    cnt_ref[...] = total.astype(jnp.int32)


def _route_call(lg_t, e_bias):
    n_tok = lg_t.shape[1]
    tn = ROUTE_TN
    tok_blk = pl.BlockSpec((TOP_K, tn), lambda i: (0, i))
    return pl.pallas_call(
        _route_kernel,
        grid=(n_tok // tn,),
        in_specs=[pl.BlockSpec((N_EXPERTS, tn), lambda i: (0, i)),
                  pl.BlockSpec((N_EXPERTS, 1), lambda i: (0, 0))],
        out_specs=[tok_blk, tok_blk, pl.BlockSpec((N_EXPERTS, 1), lambda i: (0, 0))],
        out_shape=[jax.ShapeDtypeStruct((TOP_K, n_tok), jnp.int32),
                   jax.ShapeDtypeStruct((TOP_K, n_tok), jnp.float32),
                   jax.ShapeDtypeStruct((N_EXPERTS, 1), jnp.int32)],
        scratch_shapes=[pltpu.VMEM((N_EXPERTS, 1), jnp.float32)],
        compiler_params=pltpu.CompilerParams(vmem_limit_bytes=VMEM_LIMIT,
                                             dimension_semantics=("arbitrary",)),
        name="route",
    )(lg_t, e_bias.reshape(N_EXPERTS, 1))


def _moe_weight_copies(w_hbm, wf, sem, e, slot):
    return [pltpu.make_async_copy(w.at[e], f.at[slot], sem.at[slot]) for w, f in zip(w_hbm, wf)]


def _moe_row_copy(ybuf, yk_hbm, osem, slot, r, dst_row):
    return pltpu.make_async_copy(ybuf.at[slot, pl.ds(r * PACK_CHUNKS, PACK_CHUNKS), :],
                                 yk_hbm.at[pl.ds(pl.multiple_of(dst_row, PACK_CHUNKS), PACK_CHUNKS), :],
                                 osem.at[slot])


def _moe_kernel(blk_e_ref, first_ref, wslot_ref, next1_e_ref, next2_e_ref, nvalid_ref, tok0_ref, tokn_ref,
                dstp_ref, h2p_ref, w1_hbm, w3_hbm, w2_hbm, yk_hbm, xt, ybuf, wf1, wf3, wf2, wb1, wb3, wb2,
                sem, osem):
    i = pl.program_id(0)
    nvalid = nvalid_ref[0]

    stride = MOE_XT_STRIDE
    w_hbm = (w1_hbm, w3_hbm, w2_hbm)
    wf = (wf1, wf3, wf2)
    xslot = i % 2
    oslot = i % MOE_OUT_SLOTS
    pslot = (i + MOE_OUT_SLOTS - 1) % MOE_OUT_SLOTS

    def gather_rows(tok_ref, slot):
        for r in range(MOE_TM):
            src = pl.multiple_of(tok_ref[0, 0, r], PACK_CHUNKS)
            xt[slot, pl.ds(r, PACK_CHUNKS, stride=stride), :] = h2p_ref[pl.ds(src, PACK_CHUNKS), :]

    def send_prev_rows():
        for r in range(MOE_TM):
            _moe_row_copy(ybuf, yk_hbm, osem, pslot, r, dstp_ref[0, 0, r]).start(priority=MOE_ROW_PRIORITY)

    @pl.when(i == 0)
    def _():
        ybuf[...] = jnp.zeros_like(ybuf)
        blk_rows = MOE_TM * PACK_CHUNKS
        for s in range(MOE_OUT_SLOTS):
            cp = pltpu.make_async_copy(
                ybuf.at[s], yk_hbm.at[pl.ds(yk_hbm.shape[0] - (MOE_OUT_SLOTS - s) * blk_rows, blk_rows), :],
                osem.at[s])
            cp.start()
            cp.wait()
        gather_rows(tok0_ref, 0)
        for cp in _moe_weight_copies(w_hbm, wf, sem, blk_e_ref[0], 0):
            cp.start(priority=MOE_WEIGHT_PRIORITY)

        @pl.when(next1_e_ref[0] >= 0)
        def _():
            for cp in _moe_weight_copies(w_hbm, wf, sem, next1_e_ref[0], 1):
                cp.start(priority=MOE_WEIGHT_PRIORITY)

    @pl.when((i >= 2) & (i <= nvalid + 2))
    def _():
        for r in range(MOE_TM):
            _moe_row_copy(ybuf, yk_hbm, osem, oslot, r, 0).wait()

    @pl.when(i == nvalid)
    def _():
        send_prev_rows()

    @pl.when(i < nvalid)
    def _():
        @pl.when(first_ref[i] == 1)
        def _():
            slot = wslot_ref[i]
            for cp in _moe_weight_copies(w_hbm, wf, sem, blk_e_ref[i], slot):
                cp.wait()

            @pl.when(next2_e_ref[i] >= 0)
            def _():
                ahead = (slot + MOE_W_SLOTS - 1) % MOE_W_SLOTS
                for cp in _moe_weight_copies(w_hbm, wf, sem, next2_e_ref[i], ahead):
                    cp.start(priority=MOE_WEIGHT_PRIORITY)

            wb1[...] = wf1[slot].astype(jnp.bfloat16)
            wb3[...] = wf3[slot].astype(jnp.bfloat16)
            wb2[...] = wf2[slot].astype(jnp.bfloat16)

        lo, hi = [], []
        for j in range(PACK_CHUNKS):
            word = xt[xslot, pl.ds(j * stride, MOE_TM), :]
            lo.append(lax.bitcast_convert_type(lax.shift_left(word, 16), jnp.float32).astype(jnp.bfloat16))
            hi.append(lax.bitcast_convert_type(word & _HI_MASK, jnp.float32).astype(jnp.bfloat16))
        x = jnp.concatenate(lo + hi, axis=-1)
        a = jnp.dot(x, wb1[...], preferred_element_type=jnp.float32)
        gather_rows(tokn_ref, 1 - xslot)
        b = jnp.dot(x, wb3[...], preferred_element_type=jnp.float32)
        h = (_silu(a) * b).astype(jnp.bfloat16)
        send_prev_rows()
        y = jnp.dot(h, wb2[...], preferred_element_type=jnp.float32)
        bits = lax.bitcast_convert_type(y.astype(jnp.bfloat16).astype(jnp.float32), jnp.int32)
        word = (bits[:, HALF:] & _HI_MASK) | lax.shift_right_logical(bits[:, :HALF], 16)
        for j in range(PACK_CHUNKS):
            ybuf[oslot, pl.ds(j, MOE_TM, stride=PACK_CHUNKS), :] = word[:, j * LANES:(j + 1) * LANES]


def _moe_call(blk_e, first, wslot, next1_e, next2_e, nvalid, buf_tok3, dst3, h2p, w1, w3, w2, n_rows):
    nblk = buf_tok3.shape[0]
    D = D_MODEL
    hbm = pl.BlockSpec(memory_space=pl.ANY)

    def blk_ids(shift, n):
        return pl.BlockSpec((1, 1, MOE_TM), lambda i, *_: (jnp.minimum(i + shift, n - 1), 0, 0),
                            memory_space=pltpu.SMEM)

    grid_spec = pltpu.PrefetchScalarGridSpec(
        num_scalar_prefetch=6,
        grid=(nblk + MOE_OUT_SLOTS - 1,),
        in_specs=[blk_ids(0, nblk), blk_ids(1, nblk), blk_ids(0, nblk + 1),
                  pl.BlockSpec(h2p.shape, lambda i, *_: (0, 0), pipeline_mode=pl.Buffered(1)),
                  hbm, hbm, hbm],
        out_specs=hbm,
        scratch_shapes=[pltpu.VMEM((2, PACK_CHUNKS * MOE_XT_STRIDE, LANES), jnp.int32),
                        pltpu.VMEM((MOE_OUT_SLOTS, MOE_TM * PACK_CHUNKS, LANES), jnp.int32),
                        pltpu.VMEM((MOE_W_SLOTS, D, EXPERT_DIM), jnp.float32),
                        pltpu.VMEM((MOE_W_SLOTS, D, EXPERT_DIM), jnp.float32),
                        pltpu.VMEM((MOE_W_SLOTS, EXPERT_DIM, D), jnp.float32),
                        pltpu.VMEM((D, EXPERT_DIM), jnp.bfloat16),
                        pltpu.VMEM((D, EXPERT_DIM), jnp.bfloat16),
                        pltpu.VMEM((EXPERT_DIM, D), jnp.bfloat16),
                        pltpu.SemaphoreType.DMA((MOE_W_SLOTS,)),
                        pltpu.SemaphoreType.DMA((MOE_OUT_SLOTS,))],
    )
    return pl.pallas_call(
        _moe_kernel,
        grid_spec=grid_spec,
        out_shape=jax.ShapeDtypeStruct((n_rows * PACK_CHUNKS, LANES), jnp.int32),
        compiler_params=pltpu.CompilerParams(vmem_limit_bytes=MOE_VMEM_LIMIT,
                                             dimension_semantics=("arbitrary",)),
        name="moe",
    )(blk_e, first, wslot, next1_e, next2_e, nvalid, buf_tok3, buf_tok3, dst3, h2p, w1, w3, w2)


def _combine_kernel(yk_ref, wk_ref, x1_ref, h2_ref, g2_ref, ws1_ref, ws3_ref, ws2_ref, fg_ref, o_ref):
    tm = COMBINE_TM
    h2 = h2_ref[...]
    a = jnp.dot(h2, ws1_ref[...], preferred_element_type=jnp.float32)
    b = jnp.dot(h2, ws3_ref[...], preferred_element_type=jnp.float32)
    acc = jnp.dot((_silu(a) * b).astype(jnp.bfloat16), ws2_ref[...], preferred_element_type=jnp.float32)
    wk = wk_ref[...]
    for k in range(TOP_K):
        lo, hi = [], []
        for j in range(PACK_CHUNKS):
            word = yk_ref[pl.ds(k * tm * PACK_CHUNKS + j, tm, stride=PACK_CHUNKS), :]
            lo.append(lax.bitcast_convert_type(lax.shift_left(word, 16), jnp.float32))
            hi.append(lax.bitcast_convert_type(word & _HI_MASK, jnp.float32))
        acc = acc + wk[:, k:k + 1] * jnp.concatenate(lo + hi, axis=-1)
    x = x1_ref[...] + g2_ref[0] * acc
    o_ref[...] = x * lax.rsqrt(jnp.mean(x * x, axis=-1, keepdims=True) + EPS) * fg_ref[...]


def _combine_call(yk, wk, x1, h2b, g2, ws1, ws3, ws2, fg, seq):
    n_tok, D = x1.shape
    tm = COMBINE_TM
    per_seq = seq // tm
    row = lambda i: (i, 0)
    const = lambda i: (0, 0)
    n_steps = n_tok // tm
    return pl.pallas_call(
        _combine_kernel,
        grid=(n_steps,),
        in_specs=[pl.BlockSpec((TOP_K * tm * PACK_CHUNKS, LANES), row),
                  pl.BlockSpec((tm, TOP_K), row),
                  pl.BlockSpec((tm, D), row),
                  pl.BlockSpec((tm, D), row),
                  pl.BlockSpec((1, 1, D), lambda i: (i // per_seq, 0, 0)),
                  pl.BlockSpec((D, SHARED_DIM), const),
                  pl.BlockSpec((D, SHARED_DIM), const),
                  pl.BlockSpec((SHARED_DIM, D), const),
                  pl.BlockSpec((1, D), const)],
        out_specs=pl.BlockSpec((tm, D), row),
        out_shape=jax.ShapeDtypeStruct((n_tok, D), jnp.float32),
        compiler_params=pltpu.CompilerParams(vmem_limit_bytes=VMEM_LIMIT),
        name="combine",
    )(yk, wk, x1, h2b, g2, ws1, ws3, ws2, fg)


def _pair_heads(a, axis):
    shape = a.shape
    a = a.reshape(shape[:axis] + (SWA_KV_HEADS, SWA_GROUP, HEAD_DIM) + shape[axis + 1:])
    a = jnp.swapaxes(a, axis, axis + 1)
    return a.reshape(shape)


def _bias_vectors(rel_table, seq):
    vec = rel_table.astype(jnp.float32)[_t5_bucket(jnp.arange(seq, dtype=jnp.int32))]
    m = jnp.arange(2 * SWA_BLOCK)
    d_a = jnp.clip(((-m) % (2 * SWA_BLOCK)) - SWA_BLOCK, 0, seq - 1)
    tab_sw = vec[d_a][:, :SWA_Q_HEADS].T
    nb = seq // MOBA_BLOCK
    d_b = jnp.clip(jnp.arange(nb)[:, None] * MOBA_BLOCK - MOBA_BLOCK + jnp.arange(2 * MOBA_BLOCK)[None, :],
                   0, seq - 1)
    tab_mb = jnp.transpose(vec[d_b][..., SWA_Q_HEADS:], (2, 0, 1))
    return tab_sw, tab_mb


def _route_plan(eidx_t, counts, n_tok):
    A = n_tok * TOP_K
    P = A + N_EXPERTS * MOE_TM
    nblk = P // MOE_TM
    counts = counts.reshape(N_EXPERTS)
    pcounts = (counts + MOE_TM - 1) // MOE_TM * MOE_TM
    pends = jnp.cumsum(pcounts)
    pstarts = pends - pcounts
    blk_start = jnp.arange(nblk, dtype=jnp.int32) * MOE_TM
    blk_e = jnp.minimum(jnp.sum((pends[None, :] <= blk_start[:, None]).astype(jnp.int32), axis=1),
                        N_EXPERTS - 1).astype(jnp.int32)
    nvalid = (pends[-1] // MOE_TM).astype(jnp.int32).reshape(1)
    first = jnp.concatenate([jnp.ones((1,), jnp.int32), (blk_e[1:] != blk_e[:-1]).astype(jnp.int32)])
    wslot = ((jnp.cumsum(first) - 1) % MOE_W_SLOTS).astype(jnp.int32)
    run_end = pends[blk_e] // MOE_TM

    def expert_at(blk):
        return jnp.where(blk < nvalid[0], blk_e[jnp.minimum(blk, nblk - 1)], -1).astype(jnp.int32)

    next1_e = expert_at(run_end)
    next2_e = jnp.where(next1_e >= 0, expert_at(run_end[jnp.minimum(run_end, nblk - 1)]), -1)
    slot = (jnp.arange(n_tok, dtype=jnp.int32) * TOP_K)[None, :] + jnp.arange(TOP_K, dtype=jnp.int32)[:, None]
    real_keys = (eidx_t << KEY_SHIFT) | slot
    s = jnp.arange(MOE_TM, dtype=jnp.int32)[None, :]
    e = jnp.arange(N_EXPERTS, dtype=jnp.int32)[:, None]
    pad_keys = jnp.where(s < (pcounts - counts)[:, None], (e << KEY_SHIFT) | (A + s),
                         (N_EXPERTS << KEY_SHIFT) | (A + s))
    keys = jnp.sort(jnp.concatenate([real_keys.reshape(A), pad_keys.reshape(N_EXPERTS * MOE_TM)]))
    slot_sorted = keys & ((1 << KEY_SHIFT) - 1)
    real = slot_sorted < A
    tok = slot_sorted // TOP_K
    buf_tok = jnp.where(real, tok, 0).astype(jnp.int32)
    pick = slot_sorted % TOP_K
    dst = ((tok // COMBINE_TM) * TOP_K + pick) * COMBINE_TM + tok % COMBINE_TM
    row = jnp.arange(P, dtype=jnp.int32)
    spare = A + ((row // MOE_TM) % 2) * MOE_TM + row % MOE_TM
    dst = jnp.where(real, dst, spare).astype(jnp.int32)
    dummy = A + 2 * MOE_TM + jnp.arange(MOE_TM, dtype=jnp.int32)
    dst = jnp.concatenate([dummy, dst]) * PACK_CHUNKS
    blocks = (blk_e, first, wslot, next1_e, next2_e, nvalid)
    return ((buf_tok * PACK_CHUNKS).reshape(nblk, 1, MOE_TM), dst.reshape(nblk + 1, 1, MOE_TM), blocks)


def kernel(x, c, w_ada, b_ada, norm1_g, w_in, sinks, rel_table, out_norm_a, out_norm_b, w_out, norm2_g,
           w_router, e_bias, w1, w3, w2, ws1, ws3, ws2, final_g):
    B, S, D = x.shape
    assert w_ada.shape[0] == 1, "single-layer stack only"
    assert D == D_MODEL and S % INPROJ_TM == 0 and S % MOBA_BLOCK == 0
    assert B * S * TOP_K + MOE_TM <= (1 << KEY_SHIFT)
    n_tok = B * S
    bf16 = jnp.bfloat16
    drop = lambda a: a.reshape(a.shape[1:])
    tab_sw, tab_mb = _bias_vectors(rel_table, S)
    x2 = x.reshape(n_tok, D)

    mod = _ada_call(c, drop(w_ada), drop(b_ada))
    sh1, sc1, g1, sh2, sc2, g2 = [m.reshape(B, 1, D) for m in jnp.split(mod, 6, axis=-1)]
    w_in2 = drop(w_in)
    k_cols = jnp.concatenate([jnp.ones((SWA_WIDTH,)), jnp.full((SWA_KV_WIDTH,), ATTN_SCALE),
                              jnp.ones((SWA_KV_WIDTH + MOBA_WIDTH,)), jnp.full((MOBA_WIDTH,), ATTN_SCALE),
                              jnp.ones((MOBA_WIDTH,))]).astype(jnp.float32)
    w_in_p = (jnp.concatenate([_pair_heads(w_in2[:, :SWA_WIDTH], 1), w_in2[:, SWA_WIDTH:]], axis=1)
              * k_cols[None, :]).astype(bf16)
    qa, ka, va, qb, kb, vb = _inproj_call(x2, norm1_g.reshape(1, D), sh1, sc1, w_in_p, S)
    sink_col = jnp.repeat(sinks.reshape(SWA_Q_HEADS).astype(jnp.float32), SWA_BLOCK).reshape(
        SWA_KV_HEADS, SWA_GROUP * SWA_BLOCK, 1)
    ya = _swa_call(qa.reshape(B, S, SWA_WIDTH), ka.reshape(B, S, SWA_KV_WIDTH),
                   va.reshape(B, S, SWA_KV_WIDTH), tab_sw, sink_col)
    yb = _moba_call(qb.reshape(B, S, MOBA_WIDTH), kb.reshape(B, S, MOBA_WIDTH),
                    vb.reshape(B, S, MOBA_WIDTH), tab_mb)
    w_rt = drop(w_router).T
    wr_hi = w_rt.astype(bf16)
    wr_lo = (w_rt - wr_hi.astype(jnp.float32)).astype(bf16)
    w_out2 = drop(w_out)
    w_out_p = jnp.concatenate([_pair_heads(w_out2[:SWA_WIDTH], 0), w_out2[SWA_WIDTH:]], axis=0).astype(bf16)
    x1, h2p, h2b, lg_t = _outproj_call(
        ya.reshape(n_tok, SWA_WIDTH), yb.reshape(n_tok, MOBA_WIDTH), x2,
        _pair_heads(out_norm_a.reshape(SWA_WIDTH), 0).reshape(1, SWA_WIDTH), out_norm_b.reshape(1, MOBA_WIDTH),
        w_out_p, g1, norm2_g.reshape(1, D), sh2, sc2, wr_hi, wr_lo, S)
    eidx_t, w_t, counts = _route_call(lg_t, e_bias.reshape(N_EXPERTS))
    buf_tok3, dst3, blocks = _route_plan(eidx_t, counts, n_tok)
    yk = _moe_call(*blocks, buf_tok3, dst3, h2p, drop(w1), drop(w3), drop(w2),
                   n_tok * TOP_K + MOE_OUT_SLOTS * MOE_TM)
    out = _combine_call(yk, w_t.T, x1, h2b, g2, drop(ws1).astype(bf16), drop(ws3).astype(bf16),
                        drop(ws2).astype(bf16), final_g.reshape(1, D), S)
    return out.reshape(B, S, D)
```
